```python
import jax
import jax.numpy as jnp
from jax import lax
import numpy as np

D_MODEL = 2048
BATCH = 16
SEQ = 2048
DEPTH = 1

D_FF = 5632
GLA_HEADS = 4
GLA_DK = 128
GLA_DV = 256
GLA_CHUNK = 64
GLA_GATE_RANK = 16
GLA_GATE_NORMALIZER = 16.0
MOBA_HEADS = 8
MOBA_HEAD_DIM = 128
MOBA_BLOCK = 256
MOBA_TOPK = 3
MOBA_QUERY_CHUNK = 16
ALIBI_MAX_BIAS = 8.0
NORM_EPS = 1e-6

GLA_QK_WIDTH = GLA_HEADS * GLA_DK
GLA_V_WIDTH = GLA_HEADS * GLA_DV
MOBA_WIDTH = MOBA_HEADS * MOBA_HEAD_DIM
MIX_WIDTH = GLA_V_WIDTH + MOBA_WIDTH
IN_SIZES = (GLA_QK_WIDTH, GLA_QK_WIDTH, GLA_V_WIDTH, GLA_V_WIDTH, GLA_GATE_RANK, MOBA_WIDTH, MOBA_WIDTH, MOBA_WIDTH)
IN_WIDTH = 2 * GLA_QK_WIDTH + 2 * GLA_V_WIDTH + GLA_GATE_RANK + 3 * MOBA_WIDTH

kernel_name = "hybrid_gla_moba_macaron_layer"


def rms_norm(x, w):
    xf = x.astype(jnp.float32)
    y = xf * lax.rsqrt(jnp.mean(xf * xf, axis=-1, keepdims=True) + NORM_EPS)
    return (y * w.astype(jnp.float32)).astype(x.dtype)


def swiglu(h, w_gate, w_up, w_down):
    return (jax.nn.silu(h @ w_gate) * (h @ w_up)) @ w_down


def alibi_slopes(n_heads):
    return jnp.exp2(-ALIBI_MAX_BIAS * jnp.arange(1, n_heads + 1, dtype=jnp.float32) / n_heads)


def gla_mixer(q, k, v, log_a, g, norm_w):
    B, S, H, dk = q.shape
    dv = v.shape[-1]
    C = GLA_CHUNK
    N = S // C

    def to_chunks(t):
        return t.reshape(B, N, C, H, t.shape[-1]).transpose(1, 0, 3, 2, 4)

    qc = to_chunks(q * (dk ** -0.5))
    kc = to_chunks(k)
    vc = to_chunks(v)
    Gc = jnp.cumsum(to_chunks(log_a.astype(jnp.float32)), axis=3)
    causal = jnp.tril(jnp.ones((C, C), dtype=bool))

    def step(state, inp):
        qn, kn, vn, Gn = inp
        inter = jnp.einsum('bhcd,bhde->bhce', qn * jnp.exp(Gn), state)
        diff = Gn[:, :, :, None, :] - Gn[:, :, None, :, :]
        decay = jnp.exp(jnp.where(causal[:, :, None], diff, -jnp.inf))
        scores = jnp.einsum('bhid,bhjd,bhijd->bhij', qn, kn, decay)
        intra = jnp.einsum('bhij,bhje->bhie', scores, vn)
        G_last = Gn[:, :, -1:, :]
        new_state = (jnp.exp(G_last[:, :, 0, :])[..., None] * state
                     + jnp.einsum('bhcd,bhce->bhde', kn * jnp.exp(G_last - Gn), vn))
        return new_state, inter + intra

    state0 = jnp.zeros((B, H, dk, dv), jnp.float32)
    _, o = lax.scan(step, state0, (qc, kc, vc, Gc))
    o = o.transpose(1, 0, 3, 2, 4).reshape(B, S, H, dv).astype(v.dtype)
    o = rms_norm(o, norm_w)
    return (o * jax.nn.silu(g)).reshape(B, S, H * dv)


def moba_mixer(q, k, v):
    B, S, H, hd = q.shape
    BS = MOBA_BLOCK
    QC = MOBA_QUERY_CHUNK
    NB = -(-S // BS)
    SP = NB * BS
    q = q.transpose(0, 2, 1, 3)
    k = k.transpose(0, 2, 1, 3)
    v = v.transpose(0, 2, 1, 3)
    pad = ((0, 0), (0, 0), (0, SP - S), (0, 0))
    kb = jnp.pad(k, pad).reshape(B, H, NB, BS, hd)
    vb = jnp.pad(v, pad).reshape(B, H, NB, BS, hd)

    qblk = jnp.arange(S) // BS
    kmean = jnp.mean(kb, axis=3)
    gate = jnp.einsum('bhsd,bhnd->bhsn', q, kmean).astype(jnp.float32)
    past = jnp.arange(NB)[None, :] < qblk[:, None]
    gate = jnp.where(past, gate, -jnp.inf)
    n_sel = min(MOBA_TOPK, NB)
    _, sel_idx = lax.top_k(gate, n_sel)
    sel_valid = sel_idx < qblk[:, None]

    slopes = alibi_slopes(H)
    scale = hd ** -0.5
    bi = jnp.arange(B)[:, None, None, None]
    hi = jnp.arange(H)[None, :, None, None]
    offs = jnp.arange(BS)

    def attend(c):
        start = c * QC
        q_c = lax.dynamic_slice_in_dim(q, start, QC, axis=2)
        idx_c = lax.dynamic_slice_in_dim(sel_idx, start, QC, axis=2)
        val_c = lax.dynamic_slice_in_dim(sel_valid, start, QC, axis=2)
        t = start + jnp.arange(QC)
        own = start // BS
        k_sel = kb[bi, hi, idx_c]
        v_sel = vb[bi, hi, idx_c]
        s_sel = idx_c[..., None] * BS + offs
        dist_sel = (t[:, None, None] - s_sel).astype(jnp.float32)
        l_sel = (jnp.einsum('bhqd,bhqnkd->bhqnk', q_c, k_sel).astype(jnp.float32) * scale
                 - slopes[None, :, None, None, None] * dist_sel)
        l_sel = jnp.where(val_c[..., None], l_sel, -jnp.inf)
        k_own = lax.dynamic_index_in_dim(kb, own, axis=2, keepdims=False)
        v_own = lax.dynamic_index_in_dim(vb, own, axis=2, keepdims=False)
        dist_own = t[:, None] - (own * BS + offs)[None, :]
        l_own = (jnp.einsum('bhqd,bhkd->bhqk', q_c, k_own).astype(jnp.float32) * scale
                 - slopes[None, :, None, None] * dist_own.astype(jnp.float32))
        l_own = jnp.where(dist_own >= 0, l_own, -jnp.inf)
        logits = jnp.concatenate([l_sel.reshape(B, H, QC, n_sel * BS), l_own], axis=-1)
        p = jax.nn.softmax(logits, axis=-1).astype(v.dtype)
        p_sel = p[..., :n_sel * BS].reshape(B, H, QC, n_sel, BS)
        p_own = p[..., n_sel * BS:]
        return (jnp.einsum('bhqnk,bhqnkd->bhqd', p_sel, v_sel)
                + jnp.einsum('bhqk,bhkd->bhqd', p_own, v_own))

    o = lax.map(attend, jnp.arange(S // QC))
    return o.transpose(1, 0, 3, 2, 4).reshape(B, S, H * hd)


def hybrid_mixer(h, w_in, w_decay_up, b_decay, gla_norm_w, w_out):
    B, S, _ = h.shape
    proj = h @ w_in
    points = []
    acc = 0
    for size in IN_SIZES[:-1]:
        acc += size
        points.append(acc)
    gq, gk, gv, gg, glr, mq, mk, mv = jnp.split(proj, points, axis=-1)
    log_a = jax.nn.log_sigmoid(glr @ w_decay_up + b_decay) / GLA_GATE_NORMALIZER
    gh = lambda t: t.reshape(B, S, GLA_HEADS, -1)
    mh = lambda t: t.reshape(B, S, MOBA_HEADS, -1)
    o_gla = gla_mixer(gh(gq), gh(gk), gh(gv), gh(log_a), gh(gg), gla_norm_w)
    o_moba = moba_mixer(mh(mq), mh(mk), mh(mv))
    return jnp.concatenate([o_gla, o_moba], axis=-1) @ w_out


def setup_inputs(seed: int = 0) -> dict:
    key = jax.random.key(seed)
    ks = jax.random.split(key, 18)
    L = DEPTH

    def normal(k, shape, fan_in):
        return jax.random.normal(k, shape, jnp.float32) * (fan_in ** -0.5)

    def gain(k, shape):
        return 1.0 + 0.05 * jax.random.normal(k, shape, jnp.float32)

    return {
        "x": jax.random.normal(ks[0], (BATCH, SEQ, D_MODEL), jnp.float32),
        "ffn1_pre_norm": gain(ks[1], (L, D_MODEL)),
        "ffn1_w_gate": normal(ks[2], (L, D_MODEL, D_FF), D_MODEL),
        "ffn1_w_up": normal(ks[3], (L, D_MODEL, D_FF), D_MODEL),
        "ffn1_w_down": normal(ks[4], (L, D_FF, D_MODEL), D_FF),
        "ffn1_post_norm": gain(ks[5], (L, D_MODEL)),
        "mix_pre_norm": gain(ks[6], (L, D_MODEL)),
        "w_in": normal(ks[7], (L, D_MODEL, IN_WIDTH), D_MODEL),
        "gla_w_decay_up": normal(ks[8], (L, GLA_GATE_RANK, GLA_QK_WIDTH), GLA_GATE_RANK),
        "gla_b_decay": 0.05 * jax.random.normal(ks[9], (L, GLA_QK_WIDTH), jnp.float32),
        "gla_out_norm": gain(ks[10], (L, GLA_DV)),
        "w_out": normal(ks[11], (L, MIX_WIDTH, D_MODEL), MIX_WIDTH),
        "mix_post_norm": gain(ks[12], (L, D_MODEL)),
        "ffn2_pre_norm": gain(ks[13], (L, D_MODEL)),
        "ffn2_w_gate": normal(ks[14], (L, D_MODEL, D_FF), D_MODEL),
        "ffn2_w_up": normal(ks[15], (L, D_MODEL, D_FF), D_MODEL),
        "ffn2_w_down": normal(ks[16], (L, D_FF, D_MODEL), D_FF),
        "ffn2_post_norm": gain(ks[17], (L, D_MODEL)),
    }


def reference(x, ffn1_pre_norm, ffn1_w_gate, ffn1_w_up, ffn1_w_down, ffn1_post_norm,
              mix_pre_norm, w_in, gla_w_decay_up, gla_b_decay, gla_out_norm, w_out, mix_post_norm,
              ffn2_pre_norm, ffn2_w_gate, ffn2_w_up, ffn2_w_down, ffn2_post_norm):
    h = x
    for l in range(DEPTH):
        f1 = swiglu(rms_norm(h, ffn1_pre_norm[l]), ffn1_w_gate[l], ffn1_w_up[l], ffn1_w_down[l])
        h = h + 0.5 * rms_norm(f1, ffn1_post_norm[l])
        m = hybrid_mixer(rms_norm(h, mix_pre_norm[l]), w_in[l], gla_w_decay_up[l],
                         gla_b_decay[l], gla_out_norm[l], w_out[l])
        h = h + rms_norm(m, mix_post_norm[l])
        f2 = swiglu(rms_norm(h, ffn2_pre_norm[l]), ffn2_w_gate[l], ffn2_w_up[l], ffn2_w_down[l])
        h = h + 0.5 * rms_norm(f2, ffn2_post_norm[l])
    return h
```

```python
import functools

import jax
import jax.numpy as jnp
import numpy as np
from jax import lax
from jax.experimental import pallas as pl
from jax.experimental.pallas import tpu as pltpu

F32 = jnp.float32
BF16 = jnp.bfloat16

NORM_EPS = 1e-6

GLA_HEADS = 4
GLA_DK = 128
GLA_DV = 256
GLA_CHUNK = 64
GLA_GATE_RANK = 16
GLA_GATE_NORMALIZER = 16.0
MOBA_HEADS = 8
MOBA_HEAD_DIM = 128
MOBA_BLOCK = 256
MOBA_TOPK = 3
ALIBI_MAX_BIAS = 8.0

GLA_QK_WIDTH = GLA_HEADS * GLA_DK
GLA_V_WIDTH = GLA_HEADS * GLA_DV
MOBA_WIDTH = MOBA_HEADS * MOBA_HEAD_DIM

V7X_LANES = 128
V7X_VMEM_BYTES = 64 * 1024 * 1024
VMEM_LIMIT_BYTES = V7X_VMEM_BYTES - 8 * 1024 * 1024

LR_PAD = V7X_LANES
COL_GQ = 0
COL_GK = COL_GQ + GLA_QK_WIDTH
COL_GV = COL_GK + GLA_QK_WIDTH
COL_GG = COL_GV + GLA_V_WIDTH
COL_LR = COL_GG + GLA_V_WIDTH
COL_MQ = COL_LR + LR_PAD
COL_MK = COL_MQ + MOBA_WIDTH
COL_MV = COL_MK + MOBA_WIDTH
COL_END = COL_MV + MOBA_WIDTH
IN_TILE_N = 1280
IN_PACKED_WIDTH = -(-COL_END // IN_TILE_N) * IN_TILE_N

_NT = (((1,), (1,)), ((), ()))
_TN = (((0,), (0,)), ((), ()))


def _dot(a, b):
    return jnp.dot(a, b, preferred_element_type=F32)


def _rms_norm(x, w):
    ms = jnp.mean(x * x, axis=-1, keepdims=True)
    return x * lax.rsqrt(ms + NORM_EPS) * w


def _silu(x):
    return x / (1.0 + jnp.exp(-x))


def _split3_bf16(x):
    hi = x.astype(BF16)
    r1 = x - hi.astype(F32)
    mid = r1.astype(BF16)
    lo = (r1 - mid.astype(F32)).astype(BF16)
    return hi, mid, lo


def _ffn_kernel(x_ref, pre_ref, wg_ref, wu_ref, wd_ref, post_ref, o_ref, xn_ref, acc_ref):
    j = pl.program_id(1)

    @pl.when(j == 0)
    def _():
        xn_ref[...] = _rms_norm(x_ref[...], pre_ref[...]).astype(BF16)
        acc_ref[...] = jnp.zeros_like(acc_ref)

    xn = xn_ref[...]
    g = _dot(xn, wg_ref[...])
    u = _dot(xn, wu_ref[...])
    a = (_silu(g) * u).astype(BF16)
    acc_ref[...] += _dot(a, wd_ref[...])

    @pl.when(j == pl.num_programs(1) - 1)
    def _():
        o_ref[...] = x_ref[...] + 0.5 * _rms_norm(acc_ref[...], post_ref[...])


def _ffn(x, pre_w, wg, wu, wd, post_w, *, tm, tf):
    T, D = x.shape
    F = wg.shape[1]
    assert T % tm == 0 and F % tf == 0
    return pl.pallas_call(
        _ffn_kernel,
        grid=(T // tm, F // tf),
        in_specs=[
            pl.BlockSpec((tm, D), lambda i, j: (i, 0)),
            pl.BlockSpec((1, D), lambda i, j: (0, 0)),
            pl.BlockSpec((D, tf), lambda i, j: (0, j)),
            pl.BlockSpec((D, tf), lambda i, j: (0, j)),
            pl.BlockSpec((tf, D), lambda i, j: (j, 0)),
            pl.BlockSpec((1, D), lambda i, j: (0, 0)),
        ],
        out_specs=pl.BlockSpec((tm, D), lambda i, j: (i, 0)),
        out_shape=jax.ShapeDtypeStruct((T, D), F32),
        scratch_shapes=[pltpu.VMEM((tm, D), BF16), pltpu.VMEM((tm, D), F32)],
        compiler_params=pltpu.CompilerParams(
            dimension_semantics=("parallel", "arbitrary"),
            vmem_limit_bytes=VMEM_LIMIT_BYTES),
        name="ffn",
    )(x, pre_w, wg, wu, wd, post_w)


def _in_proj_kernel(x_ref, nw_ref, w_ref, o_ref, xn_ref):
    @pl.when(pl.program_id(1) == 0)
    def _():
        xn_ref[...] = _rms_norm(x_ref[...], nw_ref[...]).astype(BF16)

    o_ref[...] = _dot(xn_ref[...], w_ref[...]).astype(o_ref.dtype)


def _in_proj(x, norm_w, w, *, tm, tn):
    T, D = x.shape
    N = w.shape[1]
    assert T % tm == 0 and N % tn == 0
    return pl.pallas_call(
        _in_proj_kernel,
        grid=(T // tm, N // tn),
        in_specs=[
            pl.BlockSpec((tm, D), lambda i, j: (i, 0)),
            pl.BlockSpec((1, D), lambda i, j: (0, 0)),
            pl.BlockSpec((D, tn), lambda i, j: (0, j)),
        ],
        out_specs=pl.BlockSpec((tm, tn), lambda i, j: (i, j)),
        out_shape=jax.ShapeDtypeStruct((T, N), BF16),
        scratch_shapes=[pltpu.VMEM((tm, D), BF16)],
        compiler_params=pltpu.CompilerParams(
            dimension_semantics=("parallel", "arbitrary"),
            vmem_limit_bytes=VMEM_LIMIT_BYTES),
        name="in_proj",
    )(x, norm_w, w)


def _gla_tables(C):
    levels = int(np.log2(C))
    idx = np.arange(C)
    sums = [np.tril(np.ones((C, C), np.float32))]
    masks = []
    for l in range(levels):
        b = 1 << l
        ref_row = (idx // (2 * b)) * (2 * b) + b - 1
        t = idx[None, :]
        i = idx[:, None]
        r = ref_row[:, None]
        m = np.where((t > r) & (t <= i), 1.0, 0.0) - np.where((t > i) & (t <= r), 1.0, 0.0)
        sums.append(m.astype(np.float32))
        second = (idx // b) % 2 == 1
        same = (idx[:, None] // (2 * b)) == (idx[None, :] // (2 * b))
        masks.append((same & second[:, None] & ~second[None, :]).astype(np.float32))
    masks.append(np.eye(C, dtype=np.float32))
    return np.concatenate(sums, 0), np.stack(masks, 0)


def _gla_kernel(q_ref, k_ref, v_ref, g_ref, lr_ref, wd_ref, bd_ref, nw_ref, sums_ref,
                masks_ref, o_ref, la_ref, state_ref, *, C):
    S = q_ref.shape[0]
    dk = q_ref.shape[1]
    levels = masks_ref.shape[0] - 1
    q_scale = dk ** -0.5

    z = _dot(lr_ref[...], wd_ref[...]) + bd_ref[...]
    log_sig = jnp.minimum(z, 0.0) - jnp.log1p(jnp.exp(-jnp.abs(z)))
    la_ref[...] = log_sig / GLA_GATE_NORMALIZER
    state_ref[...] = jnp.zeros_like(state_ref)

    row = lax.broadcasted_iota(jnp.int32, (C, dk), 0)

    def chunk(c, carry):
        r0 = pl.multiple_of(c * C, C)
        q = q_ref[pl.ds(r0, C), :].astype(F32) * q_scale
        k = k_ref[pl.ds(r0, C), :].astype(F32)
        v = v_ref[pl.ds(r0, C), :]
        la = la_ref[pl.ds(r0, C), :]

        hi, mid, lo = _split3_bf16(la)
        sums = sums_ref[...]
        e_all = _dot(sums, hi) + _dot(sums, mid) + _dot(sums, lo)
        G = e_all[0:C]
        G_last = G[C - 1:C]

        scores = jnp.zeros((C, C), F32)
        for l in range(levels):
            E = e_all[(l + 1) * C:(l + 2) * C]
            e = jnp.exp(-jnp.abs(E))
            second = ((row >> l) & 1) == 1
            qt = (q * jnp.where(second, e, 0.0)).astype(BF16)
            kt = (k * jnp.where(second, 0.0, e)).astype(BF16)
            scores += lax.dot_general(qt, kt, _NT, preferred_element_type=F32) * masks_ref[l]
        scores += lax.dot_general(q.astype(BF16), k.astype(BF16), _NT,
                                  preferred_element_type=F32) * masks_ref[levels]

        state = state_ref[...]
        inter = _dot((q * jnp.exp(G)).astype(BF16), state.astype(BF16))
        intra = _dot(scores.astype(BF16), v)
        o = inter + intra

        k_dec = (k * jnp.exp(G_last - G)).astype(BF16)
        kv = lax.dot_general(k_dec, v, _TN, preferred_element_type=F32)
        decay_col = jnp.transpose(jnp.broadcast_to(jnp.exp(G_last), (8, dk)))[:, 0:1]
        state_ref[...] = decay_col * state + kv

        gate = g_ref[pl.ds(r0, C), :].astype(F32)
        o_ref[pl.ds(r0, C), :] = (_rms_norm(o, nw_ref[...]) * _silu(gate)).astype(o_ref.dtype)
        return carry

    lax.fori_loop(0, S // C, chunk, 0)


def _gla(proj, w_decay_up, b_decay, norm_w, *, C=GLA_CHUNK):
    B, S, _ = proj.shape
    H, dk, dv = GLA_HEADS, GLA_DK, GLA_DV
    assert S % C == 0
    sums_np, masks_np = _gla_tables(C)
    sums = jnp.asarray(sums_np, BF16)
    masks = jnp.asarray(masks_np, F32)
    n_lvl = masks_np.shape[0]
    kern = functools.partial(_gla_kernel, C=C)
    return pl.pallas_call(
        kern,
        grid=(B, H),
        in_specs=[
            pl.BlockSpec((None, S, dk), lambda b, h: (b, 0, COL_GQ // dk + h)),
            pl.BlockSpec((None, S, dk), lambda b, h: (b, 0, COL_GK // dk + h)),
            pl.BlockSpec((None, S, dv), lambda b, h: (b, 0, COL_GV // dv + h)),
            pl.BlockSpec((None, S, dv), lambda b, h: (b, 0, COL_GG // dv + h)),
            pl.BlockSpec((None, S, LR_PAD), lambda b, h: (b, 0, COL_LR // LR_PAD)),
            pl.BlockSpec((LR_PAD, dk), lambda b, h: (0, h)),
            pl.BlockSpec((1, dk), lambda b, h: (0, h)),
            pl.BlockSpec((1, dv), lambda b, h: (0, 0)),
            pl.BlockSpec(sums_np.shape, lambda b, h: (0, 0)),
            pl.BlockSpec((n_lvl, C, C), lambda b, h: (0, 0, 0)),
        ],
        out_specs=pl.BlockSpec((None, S, dv), lambda b, h: (b, 0, h)),
        out_shape=jax.ShapeDtypeStruct((B, S, H * dv), BF16),
        scratch_shapes=[pltpu.VMEM((S, dk), F32), pltpu.VMEM((dk, dv), F32)],
        compiler_params=pltpu.CompilerParams(
            dimension_semantics=("parallel", "parallel"),
            vmem_limit_bytes=VMEM_LIMIT_BYTES),
        name="gla",
    )(proj, proj, proj, proj, proj, w_decay_up, b_decay, norm_w, sums, masks)


def _moba_kernel(slopes_ref, q_ref, k_ref, v_ref, o_ref, *, BS, topk):
    S, hd = q_ref.shape
    NB = S // BS
    slope = slopes_ref[pl.program_id(1)]
    scale = hd ** -0.5
    neg = -1e30

    kmean = jnp.sum(k_ref[...].astype(F32).reshape(NB, BS, hd), axis=1) * (1.0 / BS)
    hi, mid, lo = _split3_bf16(kmean)
    kparts = jnp.concatenate([hi.astype(F32), mid.astype(F32), lo.astype(F32)], 0).astype(BF16)

    rowi = lax.broadcasted_iota(jnp.int32, (BS, BS), 0)
    coli = lax.broadcasted_iota(jnp.int32, (BS, BS), 1)
    rel = (rowi - coli).astype(F32)
    causal = rowi >= coli

    for n in range(NB):
        q = q_ref[n * BS:(n + 1) * BS, :]

        sel = [None] * n
        if n > topk:
            gp = lax.dot_general(q, kparts, _NT, preferred_element_type=F32)
            gate = gp[:, 0:NB] + gp[:, NB:2 * NB] + gp[:, 2 * NB:3 * NB]
            cols = [gate[:, j:j + 1] for j in range(n)]
            for j in range(n):
                rank = jnp.zeros((BS, 1), F32)
                for jj in range(n):
                    if jj == j:
                        continue
                    ahead = (cols[jj] >= cols[j]) if jj < j else (cols[jj] > cols[j])
                    rank += ahead.astype(F32)
                sel[j] = rank < float(topk)

        def logits(j):
            s = lax.dot_general(q, k_ref[j * BS:(j + 1) * BS, :], _NT,
                                preferred_element_type=F32)
            return s * scale - slope * (rel + float((n - j) * BS))

        s = jnp.where(causal, logits(n), neg)
        m = jnp.max(s, axis=-1, keepdims=True)
        p = jnp.exp(s - m)
        l = jnp.sum(p, axis=-1, keepdims=True)
        acc = _dot(p.astype(BF16), v_ref[n * BS:(n + 1) * BS, :])
        for j in range(n):
            s = logits(j)
            if sel[j] is not None:
                s = jnp.where(sel[j], s, neg)
            m_new = jnp.maximum(m, jnp.max(s, axis=-1, keepdims=True))
            alpha = jnp.exp(m - m_new)
            p = jnp.exp(s - m_new)
            l = alpha * l + jnp.sum(p, axis=-1, keepdims=True)
            acc = alpha * acc + _dot(p.astype(BF16), v_ref[j * BS:(j + 1) * BS, :])
            m = m_new
        o_ref[n * BS:(n + 1) * BS, :] = (acc / l).astype(o_ref.dtype)


def _moba(proj, slopes):
    B, S, _ = proj.shape
    H, hd, BS = MOBA_HEADS, MOBA_HEAD_DIM, MOBA_BLOCK
    assert S % BS == 0
    kern = functools.partial(_moba_kernel, BS=BS, topk=MOBA_TOPK)
    grid_spec = pltpu.PrefetchScalarGridSpec(
        num_scalar_prefetch=1,
        grid=(B, H),
        in_specs=[
            pl.BlockSpec((None, S, hd), lambda b, h, s: (b, 0, COL_MQ // hd + h)),
            pl.BlockSpec((None, S, hd), lambda b, h, s: (b, 0, COL_MK // hd + h)),
            pl.BlockSpec((None, S, hd), lambda b, h, s: (b, 0, COL_MV // hd + h)),
        ],
        out_specs=pl.BlockSpec((None, S, hd), lambda b, h, s: (b, 0, h)),
    )
    return pl.pallas_call(
        kern,
        grid_spec=grid_spec,
        out_shape=jax.ShapeDtypeStruct((B, S, H * hd), BF16),
        compiler_params=pltpu.CompilerParams(
            dimension_semantics=("parallel", "parallel"),
            vmem_limit_bytes=VMEM_LIMIT_BYTES),
        name="moba",
    )(slopes, proj, proj, proj)


def _out_proj_kernel(og_ref, om_ref, w_ref, h_ref, nw_ref, o_ref):
    kg = og_ref.shape[1]
    m = _dot(og_ref[...], w_ref[0:kg, :]) + _dot(om_ref[...], w_ref[kg:, :])
    o_ref[...] = h_ref[...] + _rms_norm(m, nw_ref[...])


def _out_proj(o_gla, o_moba, w, h, norm_w, *, tm):
    T, D = h.shape
    kg, km = o_gla.shape[1], o_moba.shape[1]
    assert T % tm == 0
    return pl.pallas_call(
        _out_proj_kernel,
        grid=(T // tm,),
        in_specs=[
            pl.BlockSpec((tm, kg), lambda i: (i, 0)),
            pl.BlockSpec((tm, km), lambda i: (i, 0)),
            pl.BlockSpec((kg + km, D), lambda i: (0, 0)),
            pl.BlockSpec((tm, D), lambda i: (i, 0)),
            pl.BlockSpec((1, D), lambda i: (0, 0)),
        ],
        out_specs=pl.BlockSpec((tm, D), lambda i: (i, 0)),
        out_shape=jax.ShapeDtypeStruct((T, D), F32),
        compiler_params=pltpu.CompilerParams(
            dimension_semantics=("parallel",),
            vmem_limit_bytes=VMEM_LIMIT_BYTES),
        name="out_proj",
    )(o_gla, o_moba, w, h, norm_w)


def _pack_w_in(w_in):
    D = w_in.shape[0]
    lr0 = 2 * GLA_QK_WIDTH + 2 * GLA_V_WIDTH
    lr1 = lr0 + GLA_GATE_RANK
    return jnp.concatenate([
        w_in[:, :lr1],
        jnp.zeros((D, LR_PAD - GLA_GATE_RANK), w_in.dtype),
        w_in[:, lr1:],
        jnp.zeros((D, IN_PACKED_WIDTH - COL_END), w_in.dtype),
    ], axis=1).astype(BF16)


def kernel(x, ffn1_pre_norm, ffn1_w_gate, ffn1_w_up, ffn1_w_down, ffn1_post_norm, mix_pre_norm, w_in, gla_w_decay_up, gla_b_decay, gla_out_norm, w_out, mix_post_norm, ffn2_pre_norm, ffn2_w_gate, ffn2_w_up, ffn2_w_down, ffn2_post_norm):
    B, S, D = x.shape
    T = B * S
    depth = w_in.shape[0]
    slopes = jnp.exp2(-ALIBI_MAX_BIAS * jnp.arange(1, MOBA_HEADS + 1, dtype=F32) / MOBA_HEADS)
    row = lambda a: a.reshape(1, -1).astype(F32)

    h = x.reshape(T, D)
    for l in range(depth):
        h = _ffn(h, row(ffn1_pre_norm[l]), ffn1_w_gate[l].astype(BF16), ffn1_w_up[l].astype(BF16),
                 ffn1_w_down[l].astype(BF16), row(ffn1_post_norm[l]), tm=512, tf=512)

        proj = _in_proj(h, row(mix_pre_norm[l]), _pack_w_in(w_in[l]), tm=1024, tn=IN_TILE_N)
        proj = proj.reshape(B, S, IN_PACKED_WIDTH)
        w_up = jnp.concatenate(
            [gla_w_decay_up[l], jnp.zeros((LR_PAD - GLA_GATE_RANK, GLA_QK_WIDTH), F32)], 0).astype(BF16)
        o_gla = _gla(proj, w_up, row(gla_b_decay[l]), row(gla_out_norm[l]))
        o_moba = _moba(proj, slopes)
        h = _out_proj(o_gla.reshape(T, GLA_V_WIDTH), o_moba.reshape(T, MOBA_WIDTH),
                      w_out[l].astype(BF16), h, row(mix_post_norm[l]), tm=512)

        h = _ffn(h, row(ffn2_pre_norm[l]), ffn2_w_gate[l].astype(BF16), ffn2_w_up[l].astype(BF16),
                 ffn2_w_down[l].astype(BF16), row(ffn2_post_norm[l]), tm=512, tf=512)
    return h.reshape(B, S, D)
```

```python
import functools
import math

import jax
import jax.numpy as jnp
import numpy as np
from jax import lax
from jax.experimental import pallas as pl
from jax.experimental.pallas import tpu as pltpu

F32 = jnp.float32
BF16 = jnp.bfloat16

NORM_EPS = 1e-6

GLA_HEADS = 4
GLA_DK = 128
GLA_DV = 256
GLA_CHUNK = 64
GLA_CHUNKS_PER_STEP = 8
GLA_GATE_RANK = 16
GLA_GATE_NORMALIZER = 16.0
MOBA_HEADS = 8
MOBA_HEAD_DIM = 128
MOBA_BLOCK = 256
MOBA_TOPK = 3
ALIBI_MAX_BIAS = 8.0

GLA_QK_WIDTH = GLA_HEADS * GLA_DK
GLA_V_WIDTH = GLA_HEADS * GLA_DV
MOBA_WIDTH = MOBA_HEADS * MOBA_HEAD_DIM

V7X_LANES = 128
V7X_VMEM_BYTES = 64 * 1024 * 1024
VMEM_LIMIT_BYTES = V7X_VMEM_BYTES - 8 * 1024 * 1024

LR_PAD = V7X_LANES
COL_GQ = 0
COL_GK = COL_GQ + GLA_QK_WIDTH
COL_GV = COL_GK + GLA_QK_WIDTH
COL_GG = COL_GV + GLA_V_WIDTH
COL_LR = COL_GG + GLA_V_WIDTH
COL_MQ = COL_LR + LR_PAD
COL_MK = COL_MQ + MOBA_WIDTH
COL_MV = COL_MK + MOBA_WIDTH
COL_END = COL_MV + MOBA_WIDTH
IN_TILE_N = 1280
IN_PACKED_WIDTH = -(-COL_END // IN_TILE_N) * IN_TILE_N

LOG2E = math.log2(math.e)

_NT = (((1,), (1,)), ((), ()))
_TN = (((0,), (0,)), ((), ()))


def _dot(a, b):
    return jnp.dot(a, b, preferred_element_type=F32)


def _dot_nt(a, b):
    return lax.dot_general(a, b, _NT, preferred_element_type=F32)


def _dot_tn(a, b):
    return lax.dot_general(a, b, _TN, preferred_element_type=F32)


def _rms_norm(x, w):
    ms = jnp.mean(x * x, axis=-1, keepdims=True)
    return x * lax.rsqrt(ms + NORM_EPS) * w


def _silu(x):
    return x / (1.0 + jnp.exp(-x))


def _split3_bf16(x):
    hi = x.astype(BF16)
    r1 = x - hi.astype(F32)
    mid = r1.astype(BF16)
    lo = (r1 - mid.astype(F32)).astype(BF16)
    return hi, mid, lo


def _ffn_kernel(x_ref, pre_ref, wg_ref, wu_ref, wd_ref, post_ref, o_ref, xn_ref, acc_ref):
    j = pl.program_id(1)

    @pl.when(j == 0)
    def _():
        xn_ref[...] = _rms_norm(x_ref[...], pre_ref[...]).astype(BF16)
        acc_ref[...] = jnp.zeros_like(acc_ref)

    xn = xn_ref[...]
    g = _dot(xn, wg_ref[...])
    u = _dot(xn, wu_ref[...])
    a = (_silu(g) * u).astype(BF16)
    acc_ref[...] += _dot(a, wd_ref[...])

    @pl.when(j == pl.num_programs(1) - 1)
    def _():
        o_ref[...] = x_ref[...] + 0.5 * _rms_norm(acc_ref[...], post_ref[...])


def _ffn(x, pre_w, wg, wu, wd, post_w, *, tm, tf):
    T, D = x.shape
    F = wg.shape[1]
    assert T % tm == 0 and F % tf == 0
    return pl.pallas_call(
        _ffn_kernel,
        grid=(T // tm, F // tf),
        in_specs=[
            pl.BlockSpec((tm, D), lambda i, j: (i, 0)),
            pl.BlockSpec((1, D), lambda i, j: (0, 0)),
            pl.BlockSpec((D, tf), lambda i, j: (0, j)),
            pl.BlockSpec((D, tf), lambda i, j: (0, j)),
            pl.BlockSpec((tf, D), lambda i, j: (j, 0)),
            pl.BlockSpec((1, D), lambda i, j: (0, 0)),
        ],
        out_specs=pl.BlockSpec((tm, D), lambda i, j: (i, 0)),
        out_shape=jax.ShapeDtypeStruct((T, D), F32),
        scratch_shapes=[pltpu.VMEM((tm, D), BF16), pltpu.VMEM((tm, D), F32)],
        compiler_params=pltpu.CompilerParams(
            dimension_semantics=("parallel", "arbitrary"),
            vmem_limit_bytes=VMEM_LIMIT_BYTES),
        name="ffn",
    )(x, pre_w, wg, wu, wd, post_w)


def _in_proj_kernel(x_ref, nw_ref, w_ref, o_ref, xn_ref):
    @pl.when(pl.program_id(1) == 0)
    def _():
        xn_ref[...] = _rms_norm(x_ref[...], nw_ref[...]).astype(BF16)

    o_ref[...] = _dot(xn_ref[...], w_ref[...]).astype(o_ref.dtype)


def _in_proj(x, norm_w, w, *, tm, tn):
    T, D = x.shape
    N = w.shape[1]
    assert T % tm == 0 and N % tn == 0
    return pl.pallas_call(
        _in_proj_kernel,
        grid=(T // tm, N // tn),
        in_specs=[
            pl.BlockSpec((tm, D), lambda i, j: (i, 0)),
            pl.BlockSpec((1, D), lambda i, j: (0, 0)),
            pl.BlockSpec((D, tn), lambda i, j: (0, j)),
        ],
        out_specs=pl.BlockSpec((tm, tn), lambda i, j: (i, j)),
        out_shape=jax.ShapeDtypeStruct((T, N), BF16),
        scratch_shapes=[pltpu.VMEM((tm, D), BF16)],
        compiler_params=pltpu.CompilerParams(
            dimension_semantics=("parallel", "arbitrary"),
            vmem_limit_bytes=VMEM_LIMIT_BYTES),
        name="in_proj",
    )(x, norm_w, w)


def _gla_tables(C):
    levels = int(np.log2(C))
    idx = np.arange(C)
    t = idx[None, :]
    i = idx[:, None]
    sums = [np.tril(np.ones((C, C), np.float32))]
    masks = []
    for l in range(levels):
        b = 1 << l
        r = ((idx // (2 * b)) * (2 * b) + b - 1)[:, None]
        m = np.where((t > r) & (t <= i), 1.0, 0.0) + np.where((t > i) & (t <= r), 1.0, 0.0)
        sums.append(m.astype(np.float32))
        second = (idx // b) % 2 == 1
        same = (i // (2 * b)) == (t // (2 * b))
        masks.append((same & second[:, None] & ~second[None, :]).astype(np.float32))
    masks.append(np.eye(C, dtype=np.float32))
    sums = np.concatenate(sums, 0)
    return np.concatenate([sums, sums, sums], 1), np.stack(masks, 0)


def _gla_kernel(q_ref, k_ref, v_ref, g_ref, lr_ref, wd_ref, bd_ref, nw_ref, sums_ref,
                masks_ref, o_ref, la_ref, state_ref, *, C, U):
    S, dk = q_ref.shape
    levels = masks_ref.shape[0] - 1
    q_scale = dk ** -0.5

    z = _dot(lr_ref[...], wd_ref[...]) + bd_ref[...]
    log_sig = jnp.minimum(z, 0.0) - jnp.log(1.0 + jnp.exp(-jnp.abs(z)))
    la_ref[...] = log_sig * (1.0 / GLA_GATE_NORMALIZER)
    state_ref[...] = jnp.zeros_like(state_ref)

    def step(i, carry):
        base = i * (C * U)
        rows = [pl.multiple_of(base + u * C, C) for u in range(U)]
        qs = [q_ref[pl.ds(r0, C), :].astype(F32) * q_scale for r0 in rows]
        ks = [k_ref[pl.ds(r0, C), :].astype(F32) for r0 in rows]
        vs = [v_ref[pl.ds(r0, C), :] for r0 in rows]
        e_alls = []
        for r0 in rows:
            hi, mid, lo = _split3_bf16(la_ref[pl.ds(r0, C), :])
            e_alls.append(_dot(sums_ref[...], jnp.concatenate([hi, mid, lo], axis=0)))
        scores = [_dot_nt(q.astype(BF16), k.astype(BF16)) * masks_ref[levels]
                  for q, k in zip(qs, ks)]
        for l in range(levels):
            for u in range(U):
                e = jnp.exp(e_alls[u][(l + 1) * C:(l + 2) * C])
                scores[u] += _dot_nt((qs[u] * e).astype(BF16),
                                     (ks[u] * e).astype(BF16)) * masks_ref[l]
        parts = []
        for u in range(U):
            G = e_alls[u][0:C]
            G_last = G[C - 1:C]
            intra = _dot(scores[u].astype(BF16), vs[u])
            q_dec = (qs[u] * jnp.exp(G)).astype(BF16)
            k_dec = (ks[u] * jnp.exp(G_last - G)).astype(BF16)
            kv_t = _dot_tn(vs[u], k_dec)
            parts.append((q_dec, intra, kv_t, jnp.exp(G_last)))

        state_t = state_ref[...]
        for r0, (q_dec, intra, kv_t, decay_last) in zip(rows, parts):
            o = _dot_nt(q_dec, state_t.astype(BF16)) + intra
            gate = g_ref[pl.ds(r0, C), :].astype(F32)
            o_ref[pl.ds(r0, C), :] = (_rms_norm(o, nw_ref[...]) * _silu(gate)).astype(o_ref.dtype)
            state_t = state_t * decay_last + kv_t
        state_ref[...] = state_t
        return carry

    lax.fori_loop(0, S // (C * U), step, 0)


def _gla(proj, w_decay_up, b_decay, norm_w, *, C=GLA_CHUNK, U=GLA_CHUNKS_PER_STEP):
    B, S, _ = proj.shape
    H, dk, dv = GLA_HEADS, GLA_DK, GLA_DV
    assert S % (C * U) == 0
    sums_np, masks_np = _gla_tables(C)
    sums = jnp.asarray(sums_np, BF16)
    masks = jnp.asarray(masks_np, F32)
    kern = functools.partial(_gla_kernel, C=C, U=U)
    return pl.pallas_call(
        kern,
        grid=(B, H),
        in_specs=[
            pl.BlockSpec((None, S, dk), lambda b, h: (b, 0, COL_GQ // dk + h)),
            pl.BlockSpec((None, S, dk), lambda b, h: (b, 0, COL_GK // dk + h)),
            pl.BlockSpec((None, S, dv), lambda b, h: (b, 0, COL_GV // dv + h)),
            pl.BlockSpec((None, S, dv), lambda b, h: (b, 0, COL_GG // dv + h)),
            pl.BlockSpec((None, S, LR_PAD), lambda b, h: (b, 0, COL_LR // LR_PAD)),
            pl.BlockSpec((LR_PAD, dk), lambda b, h: (0, h)),
            pl.BlockSpec((1, dk), lambda b, h: (0, h)),
            pl.BlockSpec((1, dv), lambda b, h: (0, 0)),
            pl.BlockSpec(sums_np.shape, lambda b, h: (0, 0)),
            pl.BlockSpec(masks_np.shape, lambda b, h: (0, 0, 0)),
        ],
        out_specs=pl.BlockSpec((None, S, dv), lambda b, h: (b, 0, h)),
        out_shape=jax.ShapeDtypeStruct((B, S, H * dv), BF16),
        scratch_shapes=[pltpu.VMEM((S, dk), F32), pltpu.VMEM((dv, dk), F32)],
        compiler_params=pltpu.CompilerParams(
            dimension_semantics=("parallel", "parallel"),
            vmem_limit_bytes=VMEM_LIMIT_BYTES),
        name="gla",
    )(proj, proj, proj, proj, proj, w_decay_up, b_decay, norm_w, sums, masks)


def _moba_kernel(slopes_ref, q_ref, k_ref, v_ref, o_ref, vt_ref, ka_ref, qa_ref, t_ref, p_ref,
                 *, BS, topk):
    S, hd = q_ref.shape
    NB = S // BS
    scale = hd ** -0.5
    scale2 = scale * LOG2E
    beta = slopes_ref[pl.program_id(1)] * (1.0 / scale)
    neg = -1e30

    vt_ref[...] = v_ref[...].T

    lane = lax.broadcasted_iota(jnp.int32, (S, hd), 1)
    k_off = (lax.broadcasted_iota(jnp.int32, (S, hd), 0) & (BS - 1)).astype(F32)
    ka_ref[:, 0:hd] = k_ref[...]
    ka_ref[:, hd:] = jnp.where(lane < 3, k_off, 0.0).astype(BF16)
    b_hi, b_mid, b_lo = _split3_bf16(jnp.full((S, hd), beta, F32))
    qa_ref[:, 0:hd] = q_ref[...]
    qa_ref[:, hd:] = jnp.where(lane == 0, b_hi.astype(F32), jnp.where(
        lane == 1, b_mid.astype(F32), jnp.where(lane == 2, b_lo.astype(F32), 0.0))).astype(BF16)

    kmean = jnp.sum(k_ref[...].astype(F32).reshape(NB, BS, hd), axis=1) * (1.0 / BS)
    hi, mid, lo = _split3_bf16(kmean)
    kparts = jnp.concatenate([hi.astype(F32), mid.astype(F32), lo.astype(F32)], 0).astype(BF16)

    causal = (lax.broadcasted_iota(jnp.int32, (BS, BS), 0)
              <= lax.broadcasted_iota(jnp.int32, (BS, BS), 1))

    def base(n):
        return BS * (n * (n + 1) // 2)

    def pass1(n):
        q = q_ref[n * BS:(n + 1) * BS, :]
        sel = [None] * (n + 1)
        if n > topk:
            gp = _dot_nt(kparts, q)
            gate = [gp[j:j + 1] + gp[NB + j:NB + j + 1] + gp[2 * NB + j:2 * NB + j + 1]
                    for j in range(n)]
            for j in range(n):
                rank = jnp.zeros((1, BS), F32)
                for jj in range(n):
                    if jj == j:
                        continue
                    ahead = (gate[jj] >= gate[j]) if jj < j else (gate[jj] > gate[j])
                    rank += ahead.astype(F32)
                sel[j] = rank < float(topk)

        L = (n + 1) * BS
        t_ref[base(n):base(n) + L, :] = _dot_nt(ka_ref[0:L, :], qa_ref[n * BS:(n + 1) * BS, :])
        own = slice(base(n) + n * BS, base(n) + L)
        t_ref[own, :] = jnp.where(causal, t_ref[own, :], neg)
        offsets = [beta * float((n - j) * BS) for j in range(n + 1)]
        m = None
        for j in range(n + 1):
            t = t_ref[base(n) + j * BS:base(n) + (j + 1) * BS, :]
            mj = jnp.max(t, axis=0, keepdims=True) - offsets[j]
            if sel[j] is not None:
                mj = jnp.where(sel[j], mj, neg)
            m = mj if m is None else jnp.maximum(m, mj)
        return sel, offsets, m

    def pass2(n, sel, offsets, m):
        L = (n + 1) * BS
        l = jnp.zeros((1, BS), F32)
        for j in range(n + 1):
            rows = slice(base(n) + j * BS, base(n) + (j + 1) * BS)
            c = (m + offsets[j]) * scale2
            if sel[j] is not None:
                c = jnp.where(sel[j], c, -neg)
            p = jnp.exp2(t_ref[rows, :] * scale2 - c)
            l += jnp.sum(p, axis=0, keepdims=True)
            p_ref[rows, :] = p.astype(BF16)
        acc = _dot(vt_ref[:, 0:L], p_ref[base(n):base(n) + L, :])
        o_ref[n * BS:(n + 1) * BS, :] = (acc / l).T.astype(o_ref.dtype)

    stats = pass1(0)
    for n in range(NB):
        nxt = pass1(n + 1) if n + 1 < NB else None
        pass2(n, *stats)
        stats = nxt


def _moba(proj, slopes):
    B, S, _ = proj.shape
    H, hd, BS = MOBA_HEADS, MOBA_HEAD_DIM, MOBA_BLOCK
    assert S % BS == 0
    NB = S // BS
    tile_rows = BS * (NB * (NB + 1) // 2)
    kern = functools.partial(_moba_kernel, BS=BS, topk=MOBA_TOPK)
    grid_spec = pltpu.PrefetchScalarGridSpec(
        num_scalar_prefetch=1,
        grid=(B, H),
        in_specs=[
            pl.BlockSpec((None, S, hd), lambda b, h, s: (b, 0, COL_MQ // hd + h)),
            pl.BlockSpec((None, S, hd), lambda b, h, s: (b, 0, COL_MK // hd + h)),
            pl.BlockSpec((None, S, hd), lambda b, h, s: (b, 0, COL_MV // hd + h)),
        ],
        out_specs=pl.BlockSpec((None, S, hd), lambda b, h, s: (b, 0, h)),
        scratch_shapes=[pltpu.VMEM((hd, S), BF16), pltpu.VMEM((S, 2 * hd), BF16),
                        pltpu.VMEM((S, 2 * hd), BF16), pltpu.VMEM((tile_rows, BS), F32),
                        pltpu.VMEM((tile_rows, BS), BF16)],
    )
    return pl.pallas_call(
        kern,
        grid_spec=grid_spec,
        out_shape=jax.ShapeDtypeStruct((B, S, H * hd), BF16),
        compiler_params=pltpu.CompilerParams(
            dimension_semantics=("parallel", "parallel"),
            vmem_limit_bytes=VMEM_LIMIT_BYTES),
        name="moba",
    )(slopes, proj, proj, proj)


def _out_proj_kernel(og_ref, om_ref, w_ref, h_ref, nw_ref, o_ref):
    kg = og_ref.shape[1]
    m = _dot(og_ref[...], w_ref[0:kg, :]) + _dot(om_ref[...], w_ref[kg:, :])
    o_ref[...] = h_ref[...] + _rms_norm(m, nw_ref[...])


def _out_proj(o_gla, o_moba, w, h, norm_w, *, tm):
    T, D = h.shape
    kg, km = o_gla.shape[1], o_moba.shape[1]
    assert T % tm == 0
    return pl.pallas_call(
        _out_proj_kernel,
        grid=(T // tm,),
        in_specs=[
            pl.BlockSpec((tm, kg), lambda i: (i, 0)),
            pl.BlockSpec((tm, km), lambda i: (i, 0)),
            pl.BlockSpec((kg + km, D), lambda i: (0, 0)),
            pl.BlockSpec((tm, D), lambda i: (i, 0)),
            pl.BlockSpec((1, D), lambda i: (0, 0)),
        ],
        out_specs=pl.BlockSpec((tm, D), lambda i: (i, 0)),
        out_shape=jax.ShapeDtypeStruct((T, D), F32),
        compiler_params=pltpu.CompilerParams(
            dimension_semantics=("parallel",),
            vmem_limit_bytes=VMEM_LIMIT_BYTES),
        name="out_proj",
    )(o_gla, o_moba, w, h, norm_w)


def _pack_w_in(w_in):
    D = w_in.shape[0]
    lr0 = 2 * GLA_QK_WIDTH + 2 * GLA_V_WIDTH
    lr1 = lr0 + GLA_GATE_RANK
    w = w_in.astype(BF16)
    return jnp.concatenate([
        w[:, :lr1],
        jnp.zeros((D, LR_PAD - GLA_GATE_RANK), BF16),
        w[:, lr1:],
        jnp.zeros((D, IN_PACKED_WIDTH - COL_END), BF16),
    ], axis=1)


def kernel(x, ffn1_pre_norm, ffn1_w_gate, ffn1_w_up, ffn1_w_down, ffn1_post_norm, mix_pre_norm, w_in, gla_w_decay_up, gla_b_decay, gla_out_norm, w_out, mix_post_norm, ffn2_pre_norm, ffn2_w_gate, ffn2_w_up, ffn2_w_down, ffn2_post_norm):
    B, S, D = x.shape
    T = B * S
    depth = w_in.shape[0]
    slopes = jnp.exp2(-ALIBI_MAX_BIAS * jnp.arange(1, MOBA_HEADS + 1, dtype=F32) / MOBA_HEADS)
    row = lambda a: a.reshape(1, -1).astype(F32)

    h = x.reshape(T, D)
    for l in range(depth):
        h = _ffn(h, row(ffn1_pre_norm[l]), ffn1_w_gate[l].astype(BF16), ffn1_w_up[l].astype(BF16),
                 ffn1_w_down[l].astype(BF16), row(ffn1_post_norm[l]), tm=512, tf=512)

        proj = _in_proj(h, row(mix_pre_norm[l]), _pack_w_in(w_in[l]), tm=1024, tn=IN_TILE_N)
        proj = proj.reshape(B, S, IN_PACKED_WIDTH)
        w_up = jnp.concatenate(
            [gla_w_decay_up[l], jnp.zeros((LR_PAD - GLA_GATE_RANK, GLA_QK_WIDTH), F32)], 0).astype(BF16)
        o_gla = _gla(proj, w_up, row(gla_b_decay[l]), row(gla_out_norm[l]))
        o_moba = _moba(proj, slopes)
        h = _out_proj(o_gla.reshape(T, GLA_V_WIDTH), o_moba.reshape(T, MOBA_WIDTH),
                      w_out[l].astype(BF16), h, row(mix_post_norm[l]), tm=512)

        h = _ffn(h, row(ffn2_pre_norm[l]), ffn2_w_gate[l].astype(BF16), ffn2_w_up[l].astype(BF16),
                 ffn2_w_down[l].astype(BF16), row(ffn2_post_norm[l]), tm=512, tf=512)
    return h.reshape(B, S, D)
```

```python
import functools
import math

import jax
import jax.numpy as jnp
import numpy as np
from jax import lax
from jax.experimental import pallas as pl
from jax.experimental.pallas import tpu as pltpu

F32 = jnp.float32
BF16 = jnp.bfloat16

NORM_EPS = 1e-6

GLA_HEADS = 4
GLA_DK = 128
GLA_DV = 256
GLA_CHUNK = 64
GLA_CHUNKS_PER_STEP = 8
FFN_TILES_PER_GROUP = 4
GLA_GATE_RANK = 16
GLA_GATE_NORMALIZER = 16.0
MOBA_HEADS = 8
MOBA_HEAD_DIM = 128
MOBA_BLOCK = 256
MOBA_TOPK = 3
ALIBI_MAX_BIAS = 8.0

GLA_QK_WIDTH = GLA_HEADS * GLA_DK
GLA_V_WIDTH = GLA_HEADS * GLA_DV
MOBA_WIDTH = MOBA_HEADS * MOBA_HEAD_DIM

V7X_LANES = 128
V7X_VMEM_BYTES = 64 * 1024 * 1024
VMEM_LIMIT_BYTES = V7X_VMEM_BYTES - 8 * 1024 * 1024
FFN_VMEM_LIMIT_BYTES = V7X_VMEM_BYTES - 4 * 1024 * 1024

LR_PAD = V7X_LANES
COL_GQ = 0
COL_GK = COL_GQ + GLA_QK_WIDTH
COL_GV = COL_GK + GLA_QK_WIDTH
COL_GG = COL_GV + GLA_V_WIDTH
COL_LR = COL_GG + GLA_V_WIDTH
COL_MQ = COL_LR + LR_PAD
COL_MK = COL_MQ + MOBA_WIDTH
COL_MV = COL_MK + MOBA_WIDTH
COL_END = COL_MV + MOBA_WIDTH
IN_TILE_N = 1280
IN_PACKED_WIDTH = -(-COL_END // IN_TILE_N) * IN_TILE_N

LOG2E = math.log2(math.e)

_NT = (((1,), (1,)), ((), ()))
_TN = (((0,), (0,)), ((), ()))


def _dot(a, b):
    return jnp.dot(a, b, preferred_element_type=F32)


def _dot_nt(a, b):
    return lax.dot_general(a, b, _NT, preferred_element_type=F32)


def _dot_tn(a, b):
    return lax.dot_general(a, b, _TN, preferred_element_type=F32)


def _rms_norm(x, w):
    ms = jnp.mean(x * x, axis=-1, keepdims=True)
    return x * lax.rsqrt(ms + NORM_EPS) * w


def _silu(x):
    return x / (1.0 + jnp.exp(-x))


def _split3_bf16(x):
    hi = x.astype(BF16)
    r1 = x - hi.astype(F32)
    mid = r1.astype(BF16)
    lo = (r1 - mid.astype(F32)).astype(BF16)
    return hi, mid, lo


def _ffn_weight_slot(f):
    return 2 if f == 0 else (f - 1) % 2


def _ffn_kernel(xprev_ref, xnext_ref, pre_ref, post_ref, wg_hbm, wu_hbm, wd_hbm, o_ref,
                wg_buf, wu_buf, wd_buf, sem, xn_ref, acc_ref, gu_ref, *, n_tiles):
    i = pl.program_id(0)
    n_blocks = pl.num_programs(0) - 1

    def gate_up_copies(f):
        f = f % n_tiles
        s = _ffn_weight_slot(f)
        return (pltpu.make_async_copy(wg_hbm.at[f], wg_buf.at[s], sem.at[0, s]),
                pltpu.make_async_copy(wu_hbm.at[f], wu_buf.at[s], sem.at[1, s]))

    def down_copy(f):
        f = f % n_tiles
        s = _ffn_weight_slot(f)
        return (pltpu.make_async_copy(wd_hbm.at[f], wd_buf.at[s], sem.at[2, s]),)

    def start(copies):
        for c in copies:
            c.start()

    def wait(copies):
        for c in copies:
            c.wait()

    def prologue(x_ref, part=0, parts=1):
        rows = pl.ds(part * (xn_ref.shape[0] // parts), xn_ref.shape[0] // parts)
        xn_ref[rows, :] = _rms_norm(x_ref[rows, :], pre_ref[...]).astype(BF16)

    def epilogue():
        o_ref[...] = xprev_ref[...] + 0.5 * _rms_norm(acc_ref[...], post_ref[...])

    @pl.when(i == 0)
    def _():
        start(gate_up_copies(0) + gate_up_copies(1) + gate_up_copies(2) + down_copy(0))
        prologue(xprev_ref)
        acc_ref[...] = jnp.zeros_like(acc_ref)
        wait(gate_up_copies(0) + gate_up_copies(1))

    def gate_up(f):
        s = _ffn_weight_slot(f)
        gu_ref[f % 2, 0] = _dot(xn_ref[...], wg_buf[s])
        gu_ref[f % 2, 1] = _dot(xn_ref[...], wu_buf[s])

    def down(f):
        wait(down_copy(f) + gate_up_copies(f + 2))
        start(down_copy(f + 1) + gate_up_copies(f + 3))
        if f >= n_tiles - 2:
            prologue(xnext_ref, part=f - (n_tiles - 2), parts=2)
        act = (_silu(gu_ref[f % 2, 0]) * gu_ref[f % 2, 1]).astype(BF16)
        d = _dot(act, wd_buf[_ffn_weight_slot(f)])
        if f == 0:
            acc_ref[...] = d
        else:
            acc_ref[...] += d

    for f0 in range(0, n_tiles, FFN_TILES_PER_GROUP):
        @pl.when(i < n_blocks)
        def _(f0=f0):
            if f0 == 0:
                epilogue()
                gate_up(0)
            for f in range(f0, min(f0 + FFN_TILES_PER_GROUP, n_tiles)):
                if f + 1 < n_tiles:
                    gate_up(f + 1)
                down(f)

    @pl.when(i == n_blocks)
    def _():
        epilogue()
        wait(down_copy(n_tiles) + gate_up_copies(n_tiles + 2))


def _ffn(x, pre_w, wg, wu, wd, post_w, *, tm, tf):
    T, D = x.shape
    F = wg.shape[1]
    assert T % tm == 0 and F % tf == 0
    n_tiles = F // tf
    assert n_tiles >= 3
    wg_t = wg.reshape(D, n_tiles, tf).transpose(1, 0, 2)
    wu_t = wu.reshape(D, n_tiles, tf).transpose(1, 0, 2)
    wd_t = wd.reshape(n_tiles, tf, D)
    kern = functools.partial(_ffn_kernel, n_tiles=n_tiles)
    n_blocks = T // tm
    prev_block = lambda i: (jnp.maximum(i - 1, 0), 0)
    next_block = lambda i: (jnp.minimum(i + 1, n_blocks - 1), 0)
    return pl.pallas_call(
        kern,
        grid=(n_blocks + 1,),
        in_specs=[
            pl.BlockSpec((tm, D), prev_block),
            pl.BlockSpec((tm, D), next_block),
            pl.BlockSpec((1, D), lambda i: (0, 0)),
            pl.BlockSpec((1, D), lambda i: (0, 0)),
            pl.BlockSpec(memory_space=pl.ANY),
            pl.BlockSpec(memory_space=pl.ANY),
            pl.BlockSpec(memory_space=pl.ANY),
        ],
        out_specs=pl.BlockSpec((tm, D), prev_block),
        out_shape=jax.ShapeDtypeStruct((T, D), F32),
        scratch_shapes=[pltpu.VMEM((3, D, tf), BF16), pltpu.VMEM((3, D, tf), BF16),
                        pltpu.VMEM((3, tf, D), BF16), pltpu.SemaphoreType.DMA((3, 3)),
                        pltpu.VMEM((tm, D), BF16), pltpu.VMEM((tm, D), F32),
                        pltpu.VMEM((2, 2, tm, tf), F32)],
        compiler_params=pltpu.CompilerParams(
            dimension_semantics=("arbitrary",),
            vmem_limit_bytes=FFN_VMEM_LIMIT_BYTES),
        name="ffn",
    )(x, x, pre_w, post_w, wg_t, wu_t, wd_t)


def _in_proj_kernel(x_ref, nw_ref, w_ref, o_ref, xn_ref):
    @pl.when(pl.program_id(1) == 0)
    def _():
        xn_ref[...] = _rms_norm(x_ref[...], nw_ref[...]).astype(BF16)

    o_ref[...] = _dot(xn_ref[...], w_ref[...]).astype(o_ref.dtype)


def _in_proj(x, norm_w, w, *, tm, tn):
    T, D = x.shape
    N = w.shape[1]
    assert T % tm == 0 and N % tn == 0
    return pl.pallas_call(
        _in_proj_kernel,
        grid=(T // tm, N // tn),
        in_specs=[
            pl.BlockSpec((tm, D), lambda i, j: (i, 0)),
            pl.BlockSpec((1, D), lambda i, j: (0, 0)),
            pl.BlockSpec((D, tn), lambda i, j: (0, j)),
        ],
        out_specs=pl.BlockSpec((tm, tn), lambda i, j: (i, j)),
        out_shape=jax.ShapeDtypeStruct((T, N), BF16),
        scratch_shapes=[pltpu.VMEM((tm, D), BF16)],
        compiler_params=pltpu.CompilerParams(
            dimension_semantics=("parallel", "arbitrary"),
            vmem_limit_bytes=VMEM_LIMIT_BYTES),
        name="in_proj",
    )(x, norm_w, w)


def _gla_tables(C):
    levels = int(np.log2(C))
    idx = np.arange(C)
    t = idx[None, :]
    i = idx[:, None]
    sums = [np.tril(np.ones((C, C), np.float32))]
    masks = []
    for l in range(levels):
        b = 1 << l
        r = ((idx // (2 * b)) * (2 * b) + b - 1)[:, None]
        m = np.where((t > r) & (t <= i), 1.0, 0.0) + np.where((t > i) & (t <= r), 1.0, 0.0)
        sums.append(m.astype(np.float32))
        second = (idx // b) % 2 == 1
        same = (i // (2 * b)) == (t // (2 * b))
        masks.append((same & second[:, None] & ~second[None, :]).astype(np.float32))
    masks.append(np.eye(C, dtype=np.float32))
    sums = np.concatenate(sums, 0)
    return np.concatenate([sums, sums, sums], 1), np.stack(masks, 0)


def _gla_kernel(q_ref, k_ref, v_ref, g_ref, lr_ref, wd_ref, bd_ref, nw_ref, sums_ref,
                masks_ref, o_ref, la_ref, state_ref, *, C, U):
    S, dk = q_ref.shape
    levels = masks_ref.shape[0] - 1
    q_scale = dk ** -0.5

    z = _dot(lr_ref[...], wd_ref[...]) + bd_ref[...]
    log_sig = jnp.minimum(z, 0.0) - jnp.log(1.0 + jnp.exp(-jnp.abs(z)))
    la_ref[...] = log_sig * (1.0 / GLA_GATE_NORMALIZER)
    state_ref[...] = jnp.zeros_like(state_ref)

    def step(i, carry):
        base = i * (C * U)
        rows = [pl.multiple_of(base + u * C, C) for u in range(U)]
        qs = [q_ref[pl.ds(r0, C), :].astype(F32) * q_scale for r0 in rows]
        ks = [k_ref[pl.ds(r0, C), :].astype(F32) for r0 in rows]
        vs = [v_ref[pl.ds(r0, C), :] for r0 in rows]
        e_alls = []
        for r0 in rows:
            hi, mid, lo = _split3_bf16(la_ref[pl.ds(r0, C), :])
            e_alls.append(_dot(sums_ref[...], jnp.concatenate([hi, mid, lo], axis=0)))
        scores = [_dot_nt(q.astype(BF16), k.astype(BF16)) * masks_ref[levels]
                  for q, k in zip(qs, ks)]
        for l in range(levels):
            for u in range(U):
                e = jnp.exp(e_alls[u][(l + 1) * C:(l + 2) * C])
                scores[u] += _dot_nt((qs[u] * e).astype(BF16),
                                     (ks[u] * e).astype(BF16)) * masks_ref[l]
        parts = []
        for u in range(U):
            G = e_alls[u][0:C]
            G_last = G[C - 1:C]
            intra = _dot(scores[u].astype(BF16), vs[u])
            q_dec = (qs[u] * jnp.exp(G)).astype(BF16)
            k_dec = (ks[u] * jnp.exp(G_last - G)).astype(BF16)
            kv_t = _dot_tn(vs[u], k_dec)
            parts.append((q_dec, intra, kv_t, jnp.exp(G_last)))

        state_t = state_ref[...]
        for r0, (q_dec, intra, kv_t, decay_last) in zip(rows, parts):
            o = _dot_nt(q_dec, state_t.astype(BF16)) + intra
            gate = g_ref[pl.ds(r0, C), :].astype(F32)
            o_ref[pl.ds(r0, C), :] = (_rms_norm(o, nw_ref[...]) * _silu(gate)).astype(o_ref.dtype)
            state_t = state_t * decay_last + kv_t
        state_ref[...] = state_t
        return carry

    lax.fori_loop(0, S // (C * U), step, 0)


def _gla(proj, w_decay_up, b_decay, norm_w, *, C=GLA_CHUNK, U=GLA_CHUNKS_PER_STEP):
    B, S, _ = proj.shape
    H, dk, dv = GLA_HEADS, GLA_DK, GLA_DV
    assert S % (C * U) == 0
    sums_np, masks_np = _gla_tables(C)
    sums = jnp.asarray(sums_np, BF16)
    masks = jnp.asarray(masks_np, F32)
    kern = functools.partial(_gla_kernel, C=C, U=U)
    return pl.pallas_call(
        kern,
        grid=(B, H),
        in_specs=[
            pl.BlockSpec((None, S, dk), lambda b, h: (b, 0, COL_GQ // dk + h)),
            pl.BlockSpec((None, S, dk), lambda b, h: (b, 0, COL_GK // dk + h)),
            pl.BlockSpec((None, S, dv), lambda b, h: (b, 0, COL_GV // dv + h)),
            pl.BlockSpec((None, S, dv), lambda b, h: (b, 0, COL_GG // dv + h)),
            pl.BlockSpec((None, S, LR_PAD), lambda b, h: (b, 0, COL_LR // LR_PAD)),
            pl.BlockSpec((LR_PAD, dk), lambda b, h: (0, h)),
            pl.BlockSpec((1, dk), lambda b, h: (0, h)),
            pl.BlockSpec((1, dv), lambda b, h: (0, 0)),
            pl.BlockSpec(sums_np.shape, lambda b, h: (0, 0)),
            pl.BlockSpec(masks_np.shape, lambda b, h: (0, 0, 0)),
        ],
        out_specs=pl.BlockSpec((None, S, dv), lambda b, h: (b, 0, h)),
        out_shape=jax.ShapeDtypeStruct((B, S, H * dv), BF16),
        scratch_shapes=[pltpu.VMEM((S, dk), F32), pltpu.VMEM((dv, dk), F32)],
        compiler_params=pltpu.CompilerParams(
            dimension_semantics=("parallel", "parallel"),
            vmem_limit_bytes=VMEM_LIMIT_BYTES),
        name="gla",
    )(proj, proj, proj, proj, proj, w_decay_up, b_decay, norm_w, sums, masks)


def _moba_kernel(slopes_ref, q_ref, k_ref, v_ref, o_ref, vt_ref, ka_ref, qa_ref, t_ref, p_ref,
                 *, BS, topk):
    S, hd = q_ref.shape
    NB = S // BS
    scale = hd ** -0.5
    scale2 = scale * LOG2E
    beta = slopes_ref[pl.program_id(1)] * (1.0 / scale)
    neg = -1e30

    vt_ref[...] = v_ref[...].T

    lane = lax.broadcasted_iota(jnp.int32, (S, hd), 1)
    k_off = (lax.broadcasted_iota(jnp.int32, (S, hd), 0) & (BS - 1)).astype(F32)
    ka_ref[:, 0:hd] = k_ref[...]
    ka_ref[:, hd:] = jnp.where(lane < 3, k_off, 0.0).astype(BF16)
    b_hi, b_mid, b_lo = _split3_bf16(jnp.full((S, hd), beta, F32))
    qa_ref[:, 0:hd] = q_ref[...]
    qa_ref[:, hd:] = jnp.where(lane == 0, b_hi.astype(F32), jnp.where(
        lane == 1, b_mid.astype(F32), jnp.where(lane == 2, b_lo.astype(F32), 0.0))).astype(BF16)

    kmean = jnp.sum(k_ref[...].astype(F32).reshape(NB, BS, hd), axis=1) * (1.0 / BS)
    hi, mid, lo = _split3_bf16(kmean)
    kparts = jnp.concatenate([hi.astype(F32), mid.astype(F32), lo.astype(F32)], 0).astype(BF16)

    causal = (lax.broadcasted_iota(jnp.int32, (BS, BS), 0)
              <= lax.broadcasted_iota(jnp.int32, (BS, BS), 1))

    def base(n):
        return BS * (n * (n + 1) // 2)

    def pass1(n):
        q = q_ref[n * BS:(n + 1) * BS, :]
        sel = [None] * (n + 1)
        if n > topk:
            gp = _dot_nt(kparts, q)
            gate = [gp[j:j + 1] + gp[NB + j:NB + j + 1] + gp[2 * NB + j:2 * NB + j + 1]
                    for j in range(n)]
            for j in range(n):
                rank = jnp.zeros((1, BS), F32)
                for jj in range(n):
                    if jj == j:
                        continue
                    ahead = (gate[jj] >= gate[j]) if jj < j else (gate[jj] > gate[j])
                    rank += ahead.astype(F32)
                sel[j] = rank < float(topk)

        L = (n + 1) * BS
        t_ref[base(n):base(n) + L, :] = _dot_nt(ka_ref[0:L, :], qa_ref[n * BS:(n + 1) * BS, :])
        own = slice(base(n) + n * BS, base(n) + L)
        t_ref[own, :] = jnp.where(causal, t_ref[own, :], neg)
        offsets = [beta * float((n - j) * BS) for j in range(n + 1)]
        m = None
        for j in range(n + 1):
            t = t_ref[base(n) + j * BS:base(n) + (j + 1) * BS, :]
            mj = jnp.max(t, axis=0, keepdims=True) - offsets[j]
            if sel[j] is not None:
                mj = jnp.where(sel[j], mj, neg)
            m = mj if m is None else jnp.maximum(m, mj)
        return sel, offsets, m

    def pass2(n, sel, offsets, m):
        L = (n + 1) * BS
        l = jnp.zeros((1, BS), F32)
        for j in range(n + 1):
            rows = slice(base(n) + j * BS, base(n) + (j + 1) * BS)
            c = (m + offsets[j]) * scale2
            if sel[j] is not None:
                c = jnp.where(sel[j], c, -neg)
            p = jnp.exp2(t_ref[rows, :] * scale2 - c)
            l += jnp.sum(p, axis=0, keepdims=True)
            p_ref[rows, :] = p.astype(BF16)
        acc = _dot(vt_ref[:, 0:L], p_ref[base(n):base(n) + L, :])
        o_ref[n * BS:(n + 1) * BS, :] = (acc / l).T.astype(o_ref.dtype)

    stats = pass1(0)
    for n in range(NB):
        nxt = pass1(n + 1) if n + 1 < NB else None
        pass2(n, *stats)
        stats = nxt


def _moba(proj, slopes):
    B, S, _ = proj.shape
    H, hd, BS = MOBA_HEADS, MOBA_HEAD_DIM, MOBA_BLOCK
    assert S % BS == 0
    NB = S // BS
    tile_rows = BS * (NB * (NB + 1) // 2)
    kern = functools.partial(_moba_kernel, BS=BS, topk=MOBA_TOPK)
    grid_spec = pltpu.PrefetchScalarGridSpec(
        num_scalar_prefetch=1,
        grid=(B, H),
        in_specs=[
            pl.BlockSpec((None, S, hd), lambda b, h, s: (b, 0, COL_MQ // hd + h)),
            pl.BlockSpec((None, S, hd), lambda b, h, s: (b, 0, COL_MK // hd + h)),
            pl.BlockSpec((None, S, hd), lambda b, h, s: (b, 0, COL_MV // hd + h)),
        ],
        out_specs=pl.BlockSpec((None, S, hd), lambda b, h, s: (b, 0, h)),
        scratch_shapes=[pltpu.VMEM((hd, S), BF16), pltpu.VMEM((S, 2 * hd), BF16),
                        pltpu.VMEM((S, 2 * hd), BF16), pltpu.VMEM((tile_rows, BS), F32),
                        pltpu.VMEM((tile_rows, BS), BF16)],
    )
    return pl.pallas_call(
        kern,
        grid_spec=grid_spec,
        out_shape=jax.ShapeDtypeStruct((B, S, H * hd), BF16),
        compiler_params=pltpu.CompilerParams(
            dimension_semantics=("parallel", "parallel"),
            vmem_limit_bytes=VMEM_LIMIT_BYTES),
        name="moba",
    )(slopes, proj, proj, proj)


def _out_proj_kernel(og_ref, om_ref, w_ref, h_ref, nw_ref, o_ref):
    kg = og_ref.shape[1]
    m = _dot(og_ref[...], w_ref[0:kg, :]) + _dot(om_ref[...], w_ref[kg:, :])
    o_ref[...] = h_ref[...] + _rms_norm(m, nw_ref[...])


def _out_proj(o_gla, o_moba, w, h, norm_w, *, tm):
    T, D = h.shape
    kg, km = o_gla.shape[1], o_moba.shape[1]
    assert T % tm == 0
    return pl.pallas_call(
        _out_proj_kernel,
        grid=(T // tm,),
        in_specs=[
            pl.BlockSpec((tm, kg), lambda i: (i, 0)),
            pl.BlockSpec((tm, km), lambda i: (i, 0)),
            pl.BlockSpec((kg + km, D), lambda i: (0, 0)),
            pl.BlockSpec((tm, D), lambda i: (i, 0)),
            pl.BlockSpec((1, D), lambda i: (0, 0)),
        ],
        out_specs=pl.BlockSpec((tm, D), lambda i: (i, 0)),
        out_shape=jax.ShapeDtypeStruct((T, D), F32),
        compiler_params=pltpu.CompilerParams(
            dimension_semantics=("parallel",),
            vmem_limit_bytes=VMEM_LIMIT_BYTES),
        name="out_proj",
    )(o_gla, o_moba, w, h, norm_w)


def _pack_w_in(w_in):
    D = w_in.shape[0]
    lr0 = 2 * GLA_QK_WIDTH + 2 * GLA_V_WIDTH
    lr1 = lr0 + GLA_GATE_RANK
    w = w_in.astype(BF16)
    return jnp.concatenate([
        w[:, :lr1],
        jnp.zeros((D, LR_PAD - GLA_GATE_RANK), BF16),
        w[:, lr1:],
        jnp.zeros((D, IN_PACKED_WIDTH - COL_END), BF16),
    ], axis=1)


def kernel(x, ffn1_pre_norm, ffn1_w_gate, ffn1_w_up, ffn1_w_down, ffn1_post_norm, mix_pre_norm, w_in, gla_w_decay_up, gla_b_decay, gla_out_norm, w_out, mix_post_norm, ffn2_pre_norm, ffn2_w_gate, ffn2_w_up, ffn2_w_down, ffn2_post_norm):
    B, S, D = x.shape
    T = B * S
    depth = w_in.shape[0]
    slopes = jnp.exp2(-ALIBI_MAX_BIAS * jnp.arange(1, MOBA_HEADS + 1, dtype=F32) / MOBA_HEADS)
    row = lambda a: a.reshape(1, -1).astype(F32)

    h = x.reshape(T, D)
    for l in range(depth):
        h = _ffn(h, row(ffn1_pre_norm[l]), ffn1_w_gate[l].astype(BF16), ffn1_w_up[l].astype(BF16),
                 ffn1_w_down[l].astype(BF16), row(ffn1_post_norm[l]), tm=512, tf=512)

        proj = _in_proj(h, row(mix_pre_norm[l]), _pack_w_in(w_in[l]), tm=1024, tn=IN_TILE_N)
        proj = proj.reshape(B, S, IN_PACKED_WIDTH)
        w_up = jnp.concatenate(
            [gla_w_decay_up[l], jnp.zeros((LR_PAD - GLA_GATE_RANK, GLA_QK_WIDTH), F32)], 0).astype(BF16)
        o_gla = _gla(proj, w_up, row(gla_b_decay[l]), row(gla_out_norm[l]))
        o_moba = _moba(proj, slopes)
        h = _out_proj(o_gla.reshape(T, GLA_V_WIDTH), o_moba.reshape(T, MOBA_WIDTH),
                      w_out[l].astype(BF16), h, row(mix_post_norm[l]), tm=512)

        h = _ffn(h, row(ffn2_pre_norm[l]), ffn2_w_gate[l].astype(BF16), ffn2_w_up[l].astype(BF16),
                 ffn2_w_down[l].astype(BF16), row(ffn2_post_norm[l]), tm=512, tf=512)
    return h.reshape(B, S, D)
```

```python
import functools
import math

import jax
import jax.numpy as jnp
import numpy as np
from jax import lax
from jax.experimental import pallas as pl
from jax.experimental.pallas import tpu as pltpu

F32 = jnp.float32
BF16 = jnp.bfloat16

NORM_EPS = 1e-6

GLA_HEADS = 4
GLA_DK = 128
GLA_DV = 256
GLA_CHUNK = 64
GLA_CHUNKS_PER_STEP = 16
FFN_TILE_M = 1024
FFN_TILE_F = 256
GLA_GATE_RANK = 16
GLA_GATE_NORMALIZER = 16.0
MOBA_HEADS = 8
MOBA_HEAD_DIM = 128
MOBA_BLOCK = 256
MOBA_TOPK = 3
ALIBI_MAX_BIAS = 8.0

GLA_QK_WIDTH = GLA_HEADS * GLA_DK
GLA_V_WIDTH = GLA_HEADS * GLA_DV
MOBA_WIDTH = MOBA_HEADS * MOBA_HEAD_DIM

V7X_LANES = 128
V7X_VMEM_BYTES = 64 * 1024 * 1024
VMEM_LIMIT_BYTES = V7X_VMEM_BYTES - 8 * 1024 * 1024
FFN_VMEM_LIMIT_BYTES = V7X_VMEM_BYTES - 4 * 1024 * 1024

LR_PAD = V7X_LANES
COL_GQ = 0
COL_GK = COL_GQ + GLA_QK_WIDTH
COL_GV = COL_GK + GLA_QK_WIDTH
COL_GG = COL_GV + GLA_V_WIDTH
COL_LR = COL_GG + GLA_V_WIDTH
COL_MQ = COL_LR + LR_PAD
COL_MK = COL_MQ + MOBA_WIDTH
COL_MV = COL_MK + MOBA_WIDTH
COL_END = COL_MV + MOBA_WIDTH
IN_TILE_N = 1280
IN_PACKED_WIDTH = -(-COL_END // IN_TILE_N) * IN_TILE_N

LOG2E = math.log2(math.e)

_NT = (((1,), (1,)), ((), ()))
_TN = (((0,), (0,)), ((), ()))


def _dot(a, b):
    return jnp.dot(a, b, preferred_element_type=F32)


def _dot_nt(a, b):
    return lax.dot_general(a, b, _NT, preferred_element_type=F32)


def _dot_tn(a, b):
    return lax.dot_general(a, b, _TN, preferred_element_type=F32)


def _rms_norm(x, w):
    ms = jnp.mean(x * x, axis=-1, keepdims=True)
    return x * lax.rsqrt(ms + NORM_EPS) * w


def _silu(x):
    return x / (1.0 + jnp.exp(-x))


def _split3_bf16(x):
    hi = x.astype(BF16)
    r1 = x - hi.astype(F32)
    mid = r1.astype(BF16)
    lo = (r1 - mid.astype(F32)).astype(BF16)
    return hi, mid, lo


def _ffn_kernel(x_ref, pre_ref, wg_ref, wu_ref, wd_ref, post_ref, o_ref, xn_ref):
    j = pl.program_id(1)

    @pl.when(j == 0)
    def _():
        xn_ref[...] = _rms_norm(x_ref[...], pre_ref[...]).astype(BF16)
        o_ref[...] = jnp.zeros_like(o_ref)

    xn = xn_ref[...]
    g = _dot(xn, wg_ref[...])
    u = _dot(xn, wu_ref[...])
    o_ref[...] += _dot((_silu(g) * u).astype(BF16), wd_ref[...])

    @pl.when(j == pl.num_programs(1) - 1)
    def _():
        o_ref[...] = x_ref[...] + 0.5 * _rms_norm(o_ref[...], post_ref[...])


def _ffn(x, pre_w, wg, wu, wd, post_w, *, tm, tf):
    T, D = x.shape
    F = wg.shape[1]
    assert T % tm == 0 and F % tf == 0
    return pl.pallas_call(
        _ffn_kernel,
        grid=(T // tm, F // tf),
        in_specs=[
            pl.BlockSpec((tm, D), lambda i, j: (i, 0)),
            pl.BlockSpec((1, D), lambda i, j: (0, 0)),
            pl.BlockSpec((D, tf), lambda i, j: (0, j)),
            pl.BlockSpec((D, tf), lambda i, j: (0, j)),
            pl.BlockSpec((tf, D), lambda i, j: (j, 0)),
            pl.BlockSpec((1, D), lambda i, j: (0, 0)),
        ],
        out_specs=pl.BlockSpec((tm, D), lambda i, j: (i, 0)),
        out_shape=jax.ShapeDtypeStruct((T, D), F32),
        scratch_shapes=[pltpu.VMEM((tm, D), BF16)],
        compiler_params=pltpu.CompilerParams(
            dimension_semantics=("parallel", "arbitrary"),
            vmem_limit_bytes=FFN_VMEM_LIMIT_BYTES),
        name="ffn",
    )(x, pre_w, wg, wu, wd, post_w)


def _in_proj_kernel(x_ref, nw_ref, w_ref, o_ref, xn_ref):
    @pl.when(pl.program_id(1) == 0)
    def _():
        xn_ref[...] = _rms_norm(x_ref[...], nw_ref[...]).astype(BF16)

    o_ref[...] = _dot(xn_ref[...], w_ref[...]).astype(o_ref.dtype)


def _in_proj(x, norm_w, w, *, tm, tn):
    T, D = x.shape
    N = w.shape[1]
    assert T % tm == 0 and N % tn == 0
    return pl.pallas_call(
        _in_proj_kernel,
        grid=(T // tm, N // tn),
        in_specs=[
            pl.BlockSpec((tm, D), lambda i, j: (i, 0)),
            pl.BlockSpec((1, D), lambda i, j: (0, 0)),
            pl.BlockSpec((D, tn), lambda i, j: (0, j)),
        ],
        out_specs=pl.BlockSpec((tm, tn), lambda i, j: (i, j)),
        out_shape=jax.ShapeDtypeStruct((T, N), BF16),
        scratch_shapes=[pltpu.VMEM((tm, D), BF16)],
        compiler_params=pltpu.CompilerParams(
            dimension_semantics=("parallel", "arbitrary"),
            vmem_limit_bytes=VMEM_LIMIT_BYTES),
        name="in_proj",
    )(x, norm_w, w)


def _gla_tables(C):
    levels = int(np.log2(C))
    idx = np.arange(C)
    t = idx[None, :]
    i = idx[:, None]
    sums = [np.tril(np.ones((C, C), np.float32))]
    masks = []
    for l in range(levels):
        b = 1 << l
        r = ((idx // (2 * b)) * (2 * b) + b - 1)[:, None]
        m = np.where((t > r) & (t <= i), 1.0, 0.0) + np.where((t > i) & (t <= r), 1.0, 0.0)
        sums.append(m.astype(np.float32))
        second = (idx // b) % 2 == 1
        same = (i // (2 * b)) == (t // (2 * b))
        masks.append((same & second[:, None] & ~second[None, :]).astype(np.float32))
    masks.append(np.eye(C, dtype=np.float32))
    sums = np.concatenate(sums, 0)
    return np.concatenate([sums, sums, sums], 1), np.stack(masks, 0)


def _gla_kernel(q_ref, k_ref, v_ref, g_ref, lr_ref, wd_ref, bd_ref, nw_ref, sums_ref,
                masks_ref, o_ref, la_ref, state_ref, *, C, U):
    S, dk = q_ref.shape
    levels = masks_ref.shape[0] - 1
    q_scale = dk ** -0.5

    z = _dot(lr_ref[...], wd_ref[...]) + bd_ref[...]
    log_sig = jnp.minimum(z, 0.0) - jnp.log(1.0 + jnp.exp(-jnp.abs(z)))
    la_ref[...] = log_sig * (1.0 / GLA_GATE_NORMALIZER)
    state_ref[...] = jnp.zeros_like(state_ref)

    def step(i, carry):
        base = i * (C * U)
        rows = [pl.multiple_of(base + u * C, C) for u in range(U)]
        qs = [q_ref[pl.ds(r0, C), :].astype(F32) * q_scale for r0 in rows]
        ks = [k_ref[pl.ds(r0, C), :].astype(F32) for r0 in rows]
        vs = [v_ref[pl.ds(r0, C), :] for r0 in rows]
        e_alls = []
        for r0 in rows:
            hi, mid, lo = _split3_bf16(la_ref[pl.ds(r0, C), :])
            e_alls.append(_dot(sums_ref[...], jnp.concatenate([hi, mid, lo], axis=0)))
        scores = [_dot_nt(q.astype(BF16), k.astype(BF16)) * masks_ref[levels]
                  for q, k in zip(qs, ks)]
        for l in range(levels):
            for u in range(U):
                e = jnp.exp(e_alls[u][(l + 1) * C:(l + 2) * C])
                scores[u] += _dot_nt((qs[u] * e).astype(BF16),
                                     (ks[u] * e).astype(BF16)) * masks_ref[l]
        parts = []
        for u in range(U):
            G = e_alls[u][0:C]
            G_last = G[C - 1:C]
            intra = _dot(scores[u].astype(BF16), vs[u])
            q_dec = (qs[u] * jnp.exp(G)).astype(BF16)
            k_dec = (ks[u] * jnp.exp(G_last - G)).astype(BF16)
            kv_t = _dot_tn(vs[u], k_dec)
            parts.append((q_dec, intra, kv_t, jnp.exp(G_last)))

        state_t = state_ref[...]
        for r0, (q_dec, intra, kv_t, decay_last) in zip(rows, parts):
            o = _dot_nt(q_dec, state_t.astype(BF16)) + intra
            gate = g_ref[pl.ds(r0, C), :].astype(F32)
            o_ref[pl.ds(r0, C), :] = (_rms_norm(o, nw_ref[...]) * _silu(gate)).astype(o_ref.dtype)
            state_t = state_t * decay_last + kv_t
        state_ref[...] = state_t
        return carry

    lax.fori_loop(0, S // (C * U), step, 0)


def _gla(proj, w_decay_up, b_decay, norm_w, *, C=GLA_CHUNK, U=GLA_CHUNKS_PER_STEP):
    B, S, _ = proj.shape
    H, dk, dv = GLA_HEADS, GLA_DK, GLA_DV
    assert S % (C * U) == 0
    sums_np, masks_np = _gla_tables(C)
    sums = jnp.asarray(sums_np, BF16)
    masks = jnp.asarray(masks_np, F32)
    kern = functools.partial(_gla_kernel, C=C, U=U)
    return pl.pallas_call(
        kern,
        grid=(B, H),
        in_specs=[
            pl.BlockSpec((None, S, dk), lambda b, h: (b, 0, COL_GQ // dk + h)),
            pl.BlockSpec((None, S, dk), lambda b, h: (b, 0, COL_GK // dk + h)),
            pl.BlockSpec((None, S, dv), lambda b, h: (b, 0, COL_GV // dv + h)),
            pl.BlockSpec((None, S, dv), lambda b, h: (b, 0, COL_GG // dv + h)),
            pl.BlockSpec((None, S, LR_PAD), lambda b, h: (b, 0, COL_LR // LR_PAD)),
            pl.BlockSpec((LR_PAD, dk), lambda b, h: (0, h)),
            pl.BlockSpec((1, dk), lambda b, h: (0, h)),
            pl.BlockSpec((1, dv), lambda b, h: (0, 0)),
            pl.BlockSpec(sums_np.shape, lambda b, h: (0, 0)),
            pl.BlockSpec(masks_np.shape, lambda b, h: (0, 0, 0)),
        ],
        out_specs=pl.BlockSpec((None, S, dv), lambda b, h: (b, 0, h)),
        out_shape=jax.ShapeDtypeStruct((B, S, H * dv), BF16),
        scratch_shapes=[pltpu.VMEM((S, dk), F32), pltpu.VMEM((dv, dk), F32)],
        compiler_params=pltpu.CompilerParams(
            dimension_semantics=("parallel", "parallel"),
            vmem_limit_bytes=VMEM_LIMIT_BYTES),
        name="gla",
    )(proj, proj, proj, proj, proj, w_decay_up, b_decay, norm_w, sums, masks)


def _moba_kernel(slopes_ref, q_ref, k_ref, v_ref, o_ref, vt_ref, ka_ref, qa_ref, t_ref, p_ref,
                 *, BS, topk):
    S, hd = q_ref.shape
    NB = S // BS
    scale = hd ** -0.5
    scale2 = scale * LOG2E
    beta = slopes_ref[pl.program_id(1)] * (1.0 / scale)
    neg = -1e30

    vt_ref[...] = v_ref[...].T

    lane = lax.broadcasted_iota(jnp.int32, (S, hd), 1)
    k_off = (lax.broadcasted_iota(jnp.int32, (S, hd), 0) & (BS - 1)).astype(F32)
    ka_ref[:, 0:hd] = k_ref[...]
    ka_ref[:, hd:] = jnp.where(lane < 3, k_off, 0.0).astype(BF16)
    b_hi, b_mid, b_lo = _split3_bf16(jnp.full((S, hd), beta, F32))
    qa_ref[:, 0:hd] = q_ref[...]
    qa_ref[:, hd:] = jnp.where(lane == 0, b_hi.astype(F32), jnp.where(
        lane == 1, b_mid.astype(F32), jnp.where(lane == 2, b_lo.astype(F32), 0.0))).astype(BF16)

    kmean = jnp.sum(k_ref[...].astype(F32).reshape(NB, BS, hd), axis=1) * (1.0 / BS)
    hi, mid, lo = _split3_bf16(kmean)
    kparts = jnp.concatenate([hi.astype(F32), mid.astype(F32), lo.astype(F32)], 0).astype(BF16)

    causal = (lax.broadcasted_iota(jnp.int32, (BS, BS), 0)
              <= lax.broadcasted_iota(jnp.int32, (BS, BS), 1))

    def base(n):
        return BS * (n * (n + 1) // 2)

    def pass1(n):
        q = q_ref[n * BS:(n + 1) * BS, :]
        sel = [None] * (n + 1)
        if n > topk:
            gp = _dot_nt(kparts, q)
            gate = [gp[j:j + 1] + gp[NB + j:NB + j + 1] + gp[2 * NB + j:2 * NB + j + 1]
                    for j in range(n)]
            for j in range(n):
                rank = jnp.zeros((1, BS), F32)
                for jj in range(n):
                    if jj == j:
                        continue
                    ahead = (gate[jj] >= gate[j]) if jj < j else (gate[jj] > gate[j])
                    rank += ahead.astype(F32)
                sel[j] = rank < float(topk)

        L = (n + 1) * BS
        t_ref[base(n):base(n) + L, :] = _dot_nt(ka_ref[0:L, :], qa_ref[n * BS:(n + 1) * BS, :])
        own = slice(base(n) + n * BS, base(n) + L)
        t_ref[own, :] = jnp.where(causal, t_ref[own, :], neg)
        offsets = [beta * float((n - j) * BS) for j in range(n + 1)]
        m = None
        for j in range(n + 1):
            t = t_ref[base(n) + j * BS:base(n) + (j + 1) * BS, :]
            mj = jnp.max(t, axis=0, keepdims=True) - offsets[j]
            if sel[j] is not None:
                mj = jnp.where(sel[j], mj, neg)
            m = mj if m is None else jnp.maximum(m, mj)
        return sel, offsets, m

    def pass2(n, sel, offsets, m):
        L = (n + 1) * BS
        l = jnp.zeros((1, BS), F32)
        for j in range(n + 1):
            rows = slice(base(n) + j * BS, base(n) + (j + 1) * BS)
            c = (m + offsets[j]) * scale2
            if sel[j] is not None:
                c = jnp.where(sel[j], c, -neg)
            p = jnp.exp2(t_ref[rows, :] * scale2 - c)
            l += jnp.sum(p, axis=0, keepdims=True)
            p_ref[rows, :] = p.astype(BF16)
        acc = _dot(vt_ref[:, 0:L], p_ref[base(n):base(n) + L, :])
        o_ref[n * BS:(n + 1) * BS, :] = (acc / l).T.astype(o_ref.dtype)

    stats = pass1(0)
    for n in range(NB):
        nxt = pass1(n + 1) if n + 1 < NB else None
        pass2(n, *stats)
        stats = nxt


def _moba(proj, slopes):
    B, S, _ = proj.shape
    H, hd, BS = MOBA_HEADS, MOBA_HEAD_DIM, MOBA_BLOCK
    assert S % BS == 0
    NB = S // BS
    tile_rows = BS * (NB * (NB + 1) // 2)
    kern = functools.partial(_moba_kernel, BS=BS, topk=MOBA_TOPK)
    grid_spec = pltpu.PrefetchScalarGridSpec(
        num_scalar_prefetch=1,
        grid=(B, H),
        in_specs=[
            pl.BlockSpec((None, S, hd), lambda b, h, s: (b, 0, COL_MQ // hd + h)),
            pl.BlockSpec((None, S, hd), lambda b, h, s: (b, 0, COL_MK // hd + h)),
            pl.BlockSpec((None, S, hd), lambda b, h, s: (b, 0, COL_MV // hd + h)),
        ],
        out_specs=pl.BlockSpec((None, S, hd), lambda b, h, s: (b, 0, h)),
        scratch_shapes=[pltpu.VMEM((hd, S), BF16), pltpu.VMEM((S, 2 * hd), BF16),
                        pltpu.VMEM((S, 2 * hd), BF16), pltpu.VMEM((tile_rows, BS), F32),
                        pltpu.VMEM((tile_rows, BS), BF16)],
    )
    return pl.pallas_call(
        kern,
        grid_spec=grid_spec,
        out_shape=jax.ShapeDtypeStruct((B, S, H * hd), BF16),
        compiler_params=pltpu.CompilerParams(
            dimension_semantics=("parallel", "parallel"),
            vmem_limit_bytes=VMEM_LIMIT_BYTES),
        name="moba",
    )(slopes, proj, proj, proj)


def _out_proj_kernel(og_ref, om_ref, w_ref, h_ref, nw_ref, o_ref):
    kg = og_ref.shape[1]
    m = _dot(og_ref[...], w_ref[0:kg, :]) + _dot(om_ref[...], w_ref[kg:, :])
    o_ref[...] = h_ref[...] + _rms_norm(m, nw_ref[...])


def _out_proj(o_gla, o_moba, w, h, norm_w, *, tm):
    T, D = h.shape
    kg, km = o_gla.shape[1], o_moba.shape[1]
    assert T % tm == 0
    return pl.pallas_call(
        _out_proj_kernel,
        grid=(T // tm,),
        in_specs=[
            pl.BlockSpec((tm, kg), lambda i: (i, 0)),
            pl.BlockSpec((tm, km), lambda i: (i, 0)),
            pl.BlockSpec((kg + km, D), lambda i: (0, 0)),
            pl.BlockSpec((tm, D), lambda i: (i, 0)),
            pl.BlockSpec((1, D), lambda i: (0, 0)),
        ],
        out_specs=pl.BlockSpec((tm, D), lambda i: (i, 0)),
        out_shape=jax.ShapeDtypeStruct((T, D), F32),
        compiler_params=pltpu.CompilerParams(
            dimension_semantics=("parallel",),
            vmem_limit_bytes=VMEM_LIMIT_BYTES),
        name="out_proj",
    )(o_gla, o_moba, w, h, norm_w)


def _pack_w_in(w_in):
    D = w_in.shape[0]
    lr0 = 2 * GLA_QK_WIDTH + 2 * GLA_V_WIDTH
    lr1 = lr0 + GLA_GATE_RANK
    w = w_in.astype(BF16)
    return jnp.concatenate([
        w[:, :lr1],
        jnp.zeros((D, LR_PAD - GLA_GATE_RANK), BF16),
        w[:, lr1:],
        jnp.zeros((D, IN_PACKED_WIDTH - COL_END), BF16),
    ], axis=1)


def kernel(x, ffn1_pre_norm, ffn1_w_gate, ffn1_w_up, ffn1_w_down, ffn1_post_norm, mix_pre_norm, w_in, gla_w_decay_up, gla_b_decay, gla_out_norm, w_out, mix_post_norm, ffn2_pre_norm, ffn2_w_gate, ffn2_w_up, ffn2_w_down, ffn2_post_norm):
    B, S, D = x.shape
    T = B * S
    depth = w_in.shape[0]
    slopes = jnp.exp2(-ALIBI_MAX_BIAS * jnp.arange(1, MOBA_HEADS + 1, dtype=F32) / MOBA_HEADS)
    row = lambda a: a.reshape(1, -1).astype(F32)

    h = x.reshape(T, D)
    for l in range(depth):
        h = _ffn(h, row(ffn1_pre_norm[l]), ffn1_w_gate[l].astype(BF16), ffn1_w_up[l].astype(BF16),
                 ffn1_w_down[l].astype(BF16), row(ffn1_post_norm[l]), tm=FFN_TILE_M, tf=FFN_TILE_F)

        proj = _in_proj(h, row(mix_pre_norm[l]), _pack_w_in(w_in[l]), tm=1024, tn=IN_TILE_N)
        proj = proj.reshape(B, S, IN_PACKED_WIDTH)
        w_up = jnp.concatenate(
            [gla_w_decay_up[l], jnp.zeros((LR_PAD - GLA_GATE_RANK, GLA_QK_WIDTH), F32)], 0).astype(BF16)
        o_gla = _gla(proj, w_up, row(gla_b_decay[l]), row(gla_out_norm[l]))
        o_moba = _moba(proj, slopes)
        h = _out_proj(o_gla.reshape(T, GLA_V_WIDTH), o_moba.reshape(T, MOBA_WIDTH),
                      w_out[l].astype(BF16), h, row(mix_post_norm[l]), tm=512)

        h = _ffn(h, row(ffn2_pre_norm[l]), ffn2_w_gate[l].astype(BF16), ffn2_w_up[l].astype(BF16),
                 ffn2_w_down[l].astype(BF16), row(ffn2_post_norm[l]), tm=FFN_TILE_M, tf=FFN_TILE_F)
    return h.reshape(B, S, D)
```

```python
import functools
import math

import jax
import jax.numpy as jnp
import numpy as np
from jax import lax
from jax.experimental import pallas as pl
from jax.experimental.pallas import tpu as pltpu

F32 = jnp.float32
BF16 = jnp.bfloat16

NORM_EPS = 1e-6

GLA_HEADS = 4
GLA_DK = 128
GLA_DV = 256
GLA_CHUNK = 64
GLA_CHUNKS_PER_STEP = 32
FFN_TILE_M = 512
FFN_TILE_F = 512
GLA_GATE_RANK = 16
GLA_GATE_NORMALIZER = 16.0
MOBA_HEADS = 8
MOBA_HEAD_DIM = 128
MOBA_BLOCK = 256
MOBA_TOPK = 3
ALIBI_MAX_BIAS = 8.0

GLA_QK_WIDTH = GLA_HEADS * GLA_DK
GLA_V_WIDTH = GLA_HEADS * GLA_DV
MOBA_WIDTH = MOBA_HEADS * MOBA_HEAD_DIM

V7X_LANES = 128
V7X_VMEM_BYTES = 64 * 1024 * 1024
VMEM_LIMIT_BYTES = V7X_VMEM_BYTES - 8 * 1024 * 1024
FFN_VMEM_LIMIT_BYTES = V7X_VMEM_BYTES - 4 * 1024 * 1024

LR_PAD = V7X_LANES
COL_GQ = 0
COL_GK = COL_GQ + GLA_QK_WIDTH
COL_GV = COL_GK + GLA_QK_WIDTH
COL_GG = COL_GV + GLA_V_WIDTH
COL_LR = COL_GG + GLA_V_WIDTH
COL_MQ = COL_LR + LR_PAD
COL_MK = COL_MQ + MOBA_WIDTH
COL_MV = COL_MK + MOBA_WIDTH
COL_END = COL_MV + MOBA_WIDTH
IN_TILE_N = 1280
IN_PACKED_WIDTH = -(-COL_END // IN_TILE_N) * IN_TILE_N

LOG2E = math.log2(math.e)

_NT = (((1,), (1,)), ((), ()))
_TN = (((0,), (0,)), ((), ()))


def _dot(a, b):
    return jnp.dot(a, b, preferred_element_type=F32)


def _dot_nt(a, b):
    return lax.dot_general(a, b, _NT, preferred_element_type=F32)


def _dot_tn(a, b):
    return lax.dot_general(a, b, _TN, preferred_element_type=F32)


def _rms_norm(x, w):
    ms = jnp.mean(x * x, axis=-1, keepdims=True)
    return x * lax.rsqrt(ms + NORM_EPS) * w


def _silu(x):
    return x / (1.0 + jnp.exp(-x))


def _split3_bf16(x):
    hi = x.astype(BF16)
    r1 = x - hi.astype(F32)
    mid = r1.astype(BF16)
    lo = (r1 - mid.astype(F32)).astype(BF16)
    return hi, mid, lo


def _ffn_kernel(x_ref, pre_ref, wg_ref, wu_ref, wd_ref, post_ref, o_ref, xn_ref):
    j = pl.program_id(1)

    @pl.when(j == 0)
    def _():
        xn_ref[...] = _rms_norm(x_ref[...], pre_ref[...]).astype(BF16)
        o_ref[...] = jnp.zeros_like(o_ref)

    xn = xn_ref[...]
    g = _dot(xn, wg_ref[...])
    u = _dot(xn, wu_ref[...])
    o_ref[...] += _dot((_silu(g) * u).astype(BF16), wd_ref[...])

    @pl.when(j == pl.num_programs(1) - 1)
    def _():
        o_ref[...] = x_ref[...] + 0.5 * _rms_norm(o_ref[...], post_ref[...])


def _ffn(x, pre_w, wg, wu, wd, post_w, *, tm, tf):
    T, D = x.shape
    F = wg.shape[1]
    assert T % tm == 0 and F % tf == 0
    return pl.pallas_call(
        _ffn_kernel,
        grid=(T // tm, F // tf),
        in_specs=[
            pl.BlockSpec((tm, D), lambda i, j: (i, 0)),
            pl.BlockSpec((1, D), lambda i, j: (0, 0)),
            pl.BlockSpec((D, tf), lambda i, j: (0, j)),
            pl.BlockSpec((D, tf), lambda i, j: (0, j)),
            pl.BlockSpec((tf, D), lambda i, j: (j, 0)),
            pl.BlockSpec((1, D), lambda i, j: (0, 0)),
        ],
        out_specs=pl.BlockSpec((tm, D), lambda i, j: (i, 0)),
        out_shape=jax.ShapeDtypeStruct((T, D), F32),
        scratch_shapes=[pltpu.VMEM((tm, D), BF16)],
        compiler_params=pltpu.CompilerParams(
            dimension_semantics=("parallel", "arbitrary"),
            vmem_limit_bytes=FFN_VMEM_LIMIT_BYTES),
        name="ffn",
    )(x, pre_w, wg, wu, wd, post_w)


def _in_proj_kernel(x_ref, nw_ref, w_ref, o_ref, xn_ref):
    @pl.when(pl.program_id(1) == 0)
    def _():
        xn_ref[...] = _rms_norm(x_ref[...], nw_ref[...]).astype(BF16)

    o_ref[...] = _dot(xn_ref[...], w_ref[...]).astype(o_ref.dtype)


def _in_proj(x, norm_w, w, *, tm, tn):
    T, D = x.shape
    N = w.shape[1]
    assert T % tm == 0 and N % tn == 0
    return pl.pallas_call(
        _in_proj_kernel,
        grid=(T // tm, N // tn),
        in_specs=[
            pl.BlockSpec((tm, D), lambda i, j: (i, 0)),
            pl.BlockSpec((1, D), lambda i, j: (0, 0)),
            pl.BlockSpec((D, tn), lambda i, j: (0, j)),
        ],
        out_specs=pl.BlockSpec((tm, tn), lambda i, j: (i, j)),
        out_shape=jax.ShapeDtypeStruct((T, N), BF16),
        scratch_shapes=[pltpu.VMEM((tm, D), BF16)],
        compiler_params=pltpu.CompilerParams(
            dimension_semantics=("parallel", "arbitrary"),
            vmem_limit_bytes=VMEM_LIMIT_BYTES),
        name="in_proj",
    )(x, norm_w, w)


def _gla_tables(C):
    levels = int(np.log2(C))
    idx = np.arange(C)
    t = idx[None, :]
    i = idx[:, None]
    sums = [np.tril(np.ones((C, C), np.float32))]
    masks = []
    for l in range(levels):
        b = 1 << l
        r = ((idx // (2 * b)) * (2 * b) + b - 1)[:, None]
        m = np.where((t > r) & (t <= i), 1.0, 0.0) + np.where((t > i) & (t <= r), 1.0, 0.0)
        sums.append(m.astype(np.float32))
        second = (idx // b) % 2 == 1
        same = (i // (2 * b)) == (t // (2 * b))
        masks.append((same & second[:, None] & ~second[None, :]).astype(np.float32))
    masks.append(np.eye(C, dtype=np.float32))
    sums = np.concatenate(sums, 0)
    return np.concatenate([sums, sums, sums], 1), np.stack(masks, 0)


def _gla_kernel(q_ref, k_ref, v_ref, g_ref, lr_ref, wd_ref, bd_ref, nw_ref, sums_ref,
                masks_ref, o_ref, la_ref, state_ref, *, C, U):
    S, dk = q_ref.shape
    levels = masks_ref.shape[0] - 1
    q_scale = dk ** -0.5

    z = _dot(lr_ref[...], wd_ref[...]) + bd_ref[...]
    log_sig = jnp.minimum(z, 0.0) - jnp.log(1.0 + jnp.exp(-jnp.abs(z)))
    la_ref[...] = log_sig * (1.0 / GLA_GATE_NORMALIZER)
    state_ref[...] = jnp.zeros_like(state_ref)

    def step(i, carry):
        base = i * (C * U)
        rows = [pl.multiple_of(base + u * C, C) for u in range(U)]
        qs = [q_ref[pl.ds(r0, C), :].astype(F32) * q_scale for r0 in rows]
        ks = [k_ref[pl.ds(r0, C), :].astype(F32) for r0 in rows]
        vs = [v_ref[pl.ds(r0, C), :] for r0 in rows]
        e_alls = []
        for r0 in rows:
            hi, mid, lo = _split3_bf16(la_ref[pl.ds(r0, C), :])
            e_alls.append(_dot(sums_ref[...], jnp.concatenate([hi, mid, lo], axis=0)))
        scores = [_dot_nt(q.astype(BF16), k.astype(BF16)) * masks_ref[levels]
                  for q, k in zip(qs, ks)]
        for l in range(levels):
            for u in range(U):
                e = jnp.exp(e_alls[u][(l + 1) * C:(l + 2) * C])
                scores[u] += _dot_nt((qs[u] * e).astype(BF16),
                                     (ks[u] * e).astype(BF16)) * masks_ref[l]
        parts = []
        for u in range(U):
            G = e_alls[u][0:C]
            G_last = G[C - 1:C]
            intra = _dot(scores[u].astype(BF16), vs[u])
            q_dec = (qs[u] * jnp.exp(G)).astype(BF16)
            k_dec = (ks[u] * jnp.exp(G_last - G)).astype(BF16)
            kv_t = _dot_tn(vs[u], k_dec)
            parts.append((q_dec, intra, kv_t, jnp.exp(G_last)))

        state_t = state_ref[...]
        for r0, (q_dec, intra, kv_t, decay_last) in zip(rows, parts):
            o = _dot_nt(q_dec, state_t.astype(BF16)) + intra
            gate = g_ref[pl.ds(r0, C), :].astype(F32)
            o_ref[pl.ds(r0, C), :] = (_rms_norm(o, nw_ref[...]) * _silu(gate)).astype(o_ref.dtype)
            state_t = state_t * decay_last + kv_t
        state_ref[...] = state_t
        return carry

    lax.fori_loop(0, S // (C * U), step, 0)


def _gla(proj, w_decay_up, b_decay, norm_w, *, C=GLA_CHUNK, U=GLA_CHUNKS_PER_STEP):
    B, S, _ = proj.shape
    H, dk, dv = GLA_HEADS, GLA_DK, GLA_DV
    assert S % (C * U) == 0
    sums_np, masks_np = _gla_tables(C)
    sums = jnp.asarray(sums_np, BF16)
    masks = jnp.asarray(masks_np, F32)
    kern = functools.partial(_gla_kernel, C=C, U=U)
    return pl.pallas_call(
        kern,
        grid=(B, H),
        in_specs=[
            pl.BlockSpec((None, S, dk), lambda b, h: (b, 0, COL_GQ // dk + h)),
            pl.BlockSpec((None, S, dk), lambda b, h: (b, 0, COL_GK // dk + h)),
            pl.BlockSpec((None, S, dv), lambda b, h: (b, 0, COL_GV // dv + h)),
            pl.BlockSpec((None, S, dv), lambda b, h: (b, 0, COL_GG // dv + h)),
            pl.BlockSpec((None, S, LR_PAD), lambda b, h: (b, 0, COL_LR // LR_PAD)),
            pl.BlockSpec((LR_PAD, dk), lambda b, h: (0, h)),
            pl.BlockSpec((1, dk), lambda b, h: (0, h)),
            pl.BlockSpec((1, dv), lambda b, h: (0, 0)),
            pl.BlockSpec(sums_np.shape, lambda b, h: (0, 0)),
            pl.BlockSpec(masks_np.shape, lambda b, h: (0, 0, 0)),
        ],
        out_specs=pl.BlockSpec((None, S, dv), lambda b, h: (b, 0, h)),
        out_shape=jax.ShapeDtypeStruct((B, S, H * dv), BF16),
        scratch_shapes=[pltpu.VMEM((S, dk), F32), pltpu.VMEM((dv, dk), F32)],
        compiler_params=pltpu.CompilerParams(
            dimension_semantics=("parallel", "parallel"),
            vmem_limit_bytes=VMEM_LIMIT_BYTES),
        name="gla",
    )(proj, proj, proj, proj, proj, w_decay_up, b_decay, norm_w, sums, masks)


def _moba_kernel(slopes_ref, q_ref, k_ref, v_ref, o_ref, vt_ref, ka_ref, qa_ref, t_ref, p_ref,
                 *, BS, topk):
    S, hd = q_ref.shape
    NB = S // BS
    scale = hd ** -0.5
    scale2 = scale * LOG2E
    beta = slopes_ref[pl.program_id(1)] * (1.0 / scale)
    neg = -1e30

    vt_ref[...] = v_ref[...].T

    lane = lax.broadcasted_iota(jnp.int32, (S, hd), 1)
    k_off = (lax.broadcasted_iota(jnp.int32, (S, hd), 0) & (BS - 1)).astype(F32)
    ka_ref[:, 0:hd] = k_ref[...]
    ka_ref[:, hd:] = jnp.where(lane < 3, k_off, 0.0).astype(BF16)
    b_hi, b_mid, b_lo = _split3_bf16(jnp.full((S, hd), beta, F32))
    qa_ref[:, 0:hd] = q_ref[...]
    qa_ref[:, hd:] = jnp.where(lane == 0, b_hi.astype(F32), jnp.where(
        lane == 1, b_mid.astype(F32), jnp.where(lane == 2, b_lo.astype(F32), 0.0))).astype(BF16)

    kmean = jnp.sum(k_ref[...].astype(F32).reshape(NB, BS, hd), axis=1) * (1.0 / BS)
    hi, mid, lo = _split3_bf16(kmean)
    kparts = jnp.concatenate([hi.astype(F32), mid.astype(F32), lo.astype(F32)], 0).astype(BF16)

    causal = (lax.broadcasted_iota(jnp.int32, (BS, BS), 0)
              <= lax.broadcasted_iota(jnp.int32, (BS, BS), 1))

    def base(n):
        return BS * (n * (n + 1) // 2)

    def pass1(n):
        q = q_ref[n * BS:(n + 1) * BS, :]
        sel = [None] * (n + 1)
        if n > topk:
            gp = _dot_nt(kparts, q)
            gate = [gp[j:j + 1] + gp[NB + j:NB + j + 1] + gp[2 * NB + j:2 * NB + j + 1]
                    for j in range(n)]
            for j in range(n):
                rank = jnp.zeros((1, BS), F32)
                for jj in range(n):
                    if jj == j:
                        continue
                    ahead = (gate[jj] >= gate[j]) if jj < j else (gate[jj] > gate[j])
                    rank += ahead.astype(F32)
                sel[j] = rank < float(topk)

        L = (n + 1) * BS
        t_ref[base(n):base(n) + L, :] = _dot_nt(ka_ref[0:L, :], qa_ref[n * BS:(n + 1) * BS, :])
        own = slice(base(n) + n * BS, base(n) + L)
        t_ref[own, :] = jnp.where(causal, t_ref[own, :], neg)
        offsets = [beta * float((n - j) * BS) for j in range(n + 1)]
        m = None
        for j in range(n + 1):
            t = t_ref[base(n) + j * BS:base(n) + (j + 1) * BS, :]
            mj = jnp.max(t, axis=0, keepdims=True) - offsets[j]
            if sel[j] is not None:
                mj = jnp.where(sel[j], mj, neg)
            m = mj if m is None else jnp.maximum(m, mj)
        return sel, offsets, m

    def pass2(n, sel, offsets, m):
        L = (n + 1) * BS
        l = jnp.zeros((1, BS), F32)
        for j in range(n + 1):
            rows = slice(base(n) + j * BS, base(n) + (j + 1) * BS)
            c = (m + offsets[j]) * scale2
            if sel[j] is not None:
                c = jnp.where(sel[j], c, -neg)
            p = jnp.exp2(t_ref[rows, :] * scale2 - c)
            l += jnp.sum(p, axis=0, keepdims=True)
            p_ref[rows, :] = p.astype(BF16)
        acc = _dot(vt_ref[:, 0:L], p_ref[base(n):base(n) + L, :])
        o_ref[n * BS:(n + 1) * BS, :] = (acc / l).T.astype(o_ref.dtype)

    stats = pass1(0)
    for n in range(NB):
        nxt = pass1(n + 1) if n + 1 < NB else None
        pass2(n, *stats)
        stats = nxt


def _moba(proj, slopes):
    B, S, _ = proj.shape
    H, hd, BS = MOBA_HEADS, MOBA_HEAD_DIM, MOBA_BLOCK
    assert S % BS == 0
    NB = S // BS
    tile_rows = BS * (NB * (NB + 1) // 2)
    kern = functools.partial(_moba_kernel, BS=BS, topk=MOBA_TOPK)
    grid_spec = pltpu.PrefetchScalarGridSpec(
        num_scalar_prefetch=1,
        grid=(B, H),
        in_specs=[
            pl.BlockSpec((None, S, hd), lambda b, h, s: (b, 0, COL_MQ // hd + h)),
            pl.BlockSpec((None, S, hd), lambda b, h, s: (b, 0, COL_MK // hd + h)),
            pl.BlockSpec((None, S, hd), lambda b, h, s: (b, 0, COL_MV // hd + h)),
        ],
        out_specs=pl.BlockSpec((None, S, hd), lambda b, h, s: (b, 0, h)),
        scratch_shapes=[pltpu.VMEM((hd, S), BF16), pltpu.VMEM((S, 2 * hd), BF16),
                        pltpu.VMEM((S, 2 * hd), BF16), pltpu.VMEM((tile_rows, BS), F32),
                        pltpu.VMEM((tile_rows, BS), BF16)],
    )
    return pl.pallas_call(
        kern,
        grid_spec=grid_spec,
        out_shape=jax.ShapeDtypeStruct((B, S, H * hd), BF16),
        compiler_params=pltpu.CompilerParams(
            dimension_semantics=("parallel", "parallel"),
            vmem_limit_bytes=VMEM_LIMIT_BYTES),
        name="moba",
    )(slopes, proj, proj, proj)


def _out_proj_kernel(og_ref, om_ref, w_ref, h_ref, nw_ref, o_ref):
    kg = og_ref.shape[1]
    m = _dot(og_ref[...], w_ref[0:kg, :]) + _dot(om_ref[...], w_ref[kg:, :])
    o_ref[...] = h_ref[...] + _rms_norm(m, nw_ref[...])


def _out_proj(o_gla, o_moba, w, h, norm_w, *, tm):
    T, D = h.shape
    kg, km = o_gla.shape[1], o_moba.shape[1]
    assert T % tm == 0
    return pl.pallas_call(
        _out_proj_kernel,
        grid=(T // tm,),
        in_specs=[
            pl.BlockSpec((tm, kg), lambda i: (i, 0)),
            pl.BlockSpec((tm, km), lambda i: (i, 0)),
            pl.BlockSpec((kg + km, D), lambda i: (0, 0)),
            pl.BlockSpec((tm, D), lambda i: (i, 0)),
            pl.BlockSpec((1, D), lambda i: (0, 0)),
        ],
        out_specs=pl.BlockSpec((tm, D), lambda i: (i, 0)),
        out_shape=jax.ShapeDtypeStruct((T, D), F32),
        compiler_params=pltpu.CompilerParams(
            dimension_semantics=("parallel",),
            vmem_limit_bytes=VMEM_LIMIT_BYTES),
        name="out_proj",
    )(o_gla, o_moba, w, h, norm_w)


def _pack_w_in(w_in):
    D = w_in.shape[0]
    lr0 = 2 * GLA_QK_WIDTH + 2 * GLA_V_WIDTH
    lr1 = lr0 + GLA_GATE_RANK
    w = w_in.astype(BF16)
    return jnp.concatenate([
        w[:, :lr1],
        jnp.zeros((D, LR_PAD - GLA_GATE_RANK), BF16),
        w[:, lr1:],
        jnp.zeros((D, IN_PACKED_WIDTH - COL_END), BF16),
    ], axis=1)


def kernel(x, ffn1_pre_norm, ffn1_w_gate, ffn1_w_up, ffn1_w_down, ffn1_post_norm, mix_pre_norm, w_in, gla_w_decay_up, gla_b_decay, gla_out_norm, w_out, mix_post_norm, ffn2_pre_norm, ffn2_w_gate, ffn2_w_up, ffn2_w_down, ffn2_post_norm):
    B, S, D = x.shape
    T = B * S
    depth = w_in.shape[0]
    slopes = jnp.exp2(-ALIBI_MAX_BIAS * jnp.arange(1, MOBA_HEADS + 1, dtype=F32) / MOBA_HEADS)
    row = lambda a: a.reshape(1, -1).astype(F32)

    h = x.reshape(T, D)
    for l in range(depth):
        h = _ffn(h, row(ffn1_pre_norm[l]), ffn1_w_gate[l].astype(BF16), ffn1_w_up[l].astype(BF16),
                 ffn1_w_down[l].astype(BF16), row(ffn1_post_norm[l]), tm=FFN_TILE_M, tf=FFN_TILE_F)

        proj = _in_proj(h, row(mix_pre_norm[l]), _pack_w_in(w_in[l]), tm=1024, tn=IN_TILE_N)
        proj = proj.reshape(B, S, IN_PACKED_WIDTH)
        w_up = jnp.concatenate(
            [gla_w_decay_up[l], jnp.zeros((LR_PAD - GLA_GATE_RANK, GLA_QK_WIDTH), F32)], 0).astype(BF16)
        o_gla = _gla(proj, w_up, row(gla_b_decay[l]), row(gla_out_norm[l]))
        o_moba = _moba(proj, slopes)
        h = _out_proj(o_gla.reshape(T, GLA_V_WIDTH), o_moba.reshape(T, MOBA_WIDTH),
                      w_out[l].astype(BF16), h, row(mix_post_norm[l]), tm=512)

        h = _ffn(h, row(ffn2_pre_norm[l]), ffn2_w_gate[l].astype(BF16), ffn2_w_up[l].astype(BF16),
                 ffn2_w_down[l].astype(BF16), row(ffn2_post_norm[l]), tm=FFN_TILE_M, tf=FFN_TILE_F)
    return h.reshape(B, S, D)
```

```python
import functools
import math

import jax
import jax.numpy as jnp
import numpy as np
from jax import lax
from jax.experimental import pallas as pl
from jax.experimental.pallas import tpu as pltpu

F32 = jnp.float32
BF16 = jnp.bfloat16

NORM_EPS = 1e-6

GLA_HEADS = 4
GLA_DK = 128
GLA_DV = 256
GLA_CHUNK = 64
GLA_CHUNKS_PER_STEP = 32
FFN_TILE_M = 512
FFN_TILE_F = 512
GLA_GATE_RANK = 16
GLA_GATE_NORMALIZER = 16.0
MOBA_HEADS = 8
MOBA_HEAD_DIM = 128
MOBA_BLOCK = 256
MOBA_TOPK = 3
MOBA_HEADS_PER_STEP = 2
ALIBI_MAX_BIAS = 8.0

GLA_QK_WIDTH = GLA_HEADS * GLA_DK
GLA_V_WIDTH = GLA_HEADS * GLA_DV
MOBA_WIDTH = MOBA_HEADS * MOBA_HEAD_DIM

V7X_LANES = 128
V7X_VMEM_BYTES = 64 * 1024 * 1024
VMEM_LIMIT_BYTES = V7X_VMEM_BYTES - 8 * 1024 * 1024
FFN_VMEM_LIMIT_BYTES = V7X_VMEM_BYTES - 4 * 1024 * 1024

LR_PAD = V7X_LANES
COL_GQ = 0
COL_GK = COL_GQ + GLA_QK_WIDTH
COL_GV = COL_GK + GLA_QK_WIDTH
COL_GG = COL_GV + GLA_V_WIDTH
COL_LR = COL_GG + GLA_V_WIDTH
COL_MQ = COL_LR + 2 * LR_PAD
COL_MK = COL_MQ + MOBA_WIDTH
COL_MV = COL_MK + MOBA_WIDTH
COL_END = COL_MV + MOBA_WIDTH
IN_TILE_N = 1280
IN_PACKED_WIDTH = -(-COL_END // IN_TILE_N) * IN_TILE_N

LOG2E = math.log2(math.e)

_NT = (((1,), (1,)), ((), ()))
_TN = (((0,), (0,)), ((), ()))


def _dot(a, b):
    return jnp.dot(a, b, preferred_element_type=F32)


def _dot_nt(a, b):
    return lax.dot_general(a, b, _NT, preferred_element_type=F32)


def _dot_tn(a, b):
    return lax.dot_general(a, b, _TN, preferred_element_type=F32)


def _rms_norm(x, w):
    ms = jnp.mean(x * x, axis=-1, keepdims=True)
    return x * lax.rsqrt(ms + NORM_EPS) * w


def _silu(x):
    return x / (1.0 + jnp.exp(-x))


def _split3_bf16(x):
    hi = x.astype(BF16)
    r1 = x - hi.astype(F32)
    mid = r1.astype(BF16)
    lo = (r1 - mid.astype(F32)).astype(BF16)
    return hi, mid, lo


def _ffn_kernel(x_ref, pre_ref, wg_ref, wu_ref, wd_ref, post_ref, o_ref, xn_ref):
    j = pl.program_id(1)

    @pl.when(j == 0)
    def _():
        xn_ref[...] = _rms_norm(x_ref[...], pre_ref[...]).astype(BF16)
        o_ref[...] = jnp.zeros_like(o_ref)

    xn = xn_ref[...]
    g = _dot(xn, wg_ref[...])
    u = _dot(xn, wu_ref[...])
    o_ref[...] += _dot((_silu(g) * u).astype(BF16), wd_ref[...])

    @pl.when(j == pl.num_programs(1) - 1)
    def _():
        o_ref[...] = x_ref[...] + 0.5 * _rms_norm(o_ref[...], post_ref[...])


def _ffn(x, pre_w, wg, wu, wd, post_w, *, tm, tf):
    T, D = x.shape
    F = wg.shape[1]
    assert T % tm == 0 and F % tf == 0
    return pl.pallas_call(
        _ffn_kernel,
        grid=(T // tm, F // tf),
        in_specs=[
            pl.BlockSpec((tm, D), lambda i, j: (i, 0)),
            pl.BlockSpec((1, D), lambda i, j: (0, 0)),
            pl.BlockSpec((D, tf), lambda i, j: (0, j)),
            pl.BlockSpec((D, tf), lambda i, j: (0, j)),
            pl.BlockSpec((tf, D), lambda i, j: (j, 0)),
            pl.BlockSpec((1, D), lambda i, j: (0, 0)),
        ],
        out_specs=pl.BlockSpec((tm, D), lambda i, j: (i, 0)),
        out_shape=jax.ShapeDtypeStruct((T, D), F32),
        scratch_shapes=[pltpu.VMEM((tm, D), BF16)],
        compiler_params=pltpu.CompilerParams(
            dimension_semantics=("parallel", "arbitrary"),
            vmem_limit_bytes=FFN_VMEM_LIMIT_BYTES),
        name="ffn",
    )(x, pre_w, wg, wu, wd, post_w)


def _in_proj_kernel(x_ref, nw_ref, w_ref, o_ref, xn_ref):
    @pl.when(pl.program_id(1) == 0)
    def _():
        xn_ref[...] = _rms_norm(x_ref[...], nw_ref[...]).astype(BF16)

    o_ref[...] = _dot(xn_ref[...], w_ref[...]).astype(o_ref.dtype)


def _in_proj(x, norm_w, w, *, tm, tn):
    T, D = x.shape
    N = w.shape[1]
    assert T % tm == 0 and N % tn == 0
    return pl.pallas_call(
        _in_proj_kernel,
        grid=(T // tm, N // tn),
        in_specs=[
            pl.BlockSpec((tm, D), lambda i, j: (i, 0)),
            pl.BlockSpec((1, D), lambda i, j: (0, 0)),
            pl.BlockSpec((D, tn), lambda i, j: (0, j)),
        ],
        out_specs=pl.BlockSpec((tm, tn), lambda i, j: (i, j)),
        out_shape=jax.ShapeDtypeStruct((T, N), BF16),
        scratch_shapes=[pltpu.VMEM((tm, D), BF16)],
        compiler_params=pltpu.CompilerParams(
            dimension_semantics=("parallel", "arbitrary"),
            vmem_limit_bytes=VMEM_LIMIT_BYTES),
        name="in_proj",
    )(x, norm_w, w)


def _gla_tables(C):
    levels = int(np.log2(C))
    idx = np.arange(C)
    t = idx[None, :]
    i = idx[:, None]
    sums = [np.tril(np.ones((C, C), np.float32))]
    masks = []
    for l in range(levels):
        b = 1 << l
        r = ((idx // (2 * b)) * (2 * b) + b - 1)[:, None]
        m = np.where((t > r) & (t <= i), 1.0, 0.0) + np.where((t > i) & (t <= r), 1.0, 0.0)
        sums.append(m.astype(np.float32))
        second = (idx // b) % 2 == 1
        same = (i // (2 * b)) == (t // (2 * b))
        masks.append((same & second[:, None] & ~second[None, :]).astype(np.float32))
    masks.append(np.eye(C, dtype=np.float32))
    sums = np.concatenate(sums, 0)
    return np.concatenate([sums, sums, sums], 1), np.stack(masks, 0)


def _gla_kernel(q_ref, k_ref, v_ref, g_ref, lr_ref, wd_ref, bd_ref, nw_ref, sums_ref,
                masks_ref, o_ref, la_ref, state_ref, *, C, U):
    S, dk = q_ref.shape
    levels = masks_ref.shape[0] - 1
    q_scale = dk ** -0.5

    z = _dot(lr_ref[...], wd_ref[...]) + bd_ref[...]
    log_sig = jnp.minimum(z, 0.0) - jnp.log(1.0 + jnp.exp(-jnp.abs(z)))
    la_ref[...] = log_sig * (1.0 / GLA_GATE_NORMALIZER)
    state_ref[...] = jnp.zeros_like(state_ref)

    def step(i, carry):
        base = i * (C * U)
        rows = [pl.multiple_of(base + u * C, C) for u in range(U)]
        qs = [q_ref[pl.ds(r0, C), :].astype(F32) * q_scale for r0 in rows]
        ks = [k_ref[pl.ds(r0, C), :].astype(F32) for r0 in rows]
        vs = [v_ref[pl.ds(r0, C), :] for r0 in rows]
        e_alls = []
        for r0 in rows:
            hi, mid, lo = _split3_bf16(la_ref[pl.ds(r0, C), :])
            e_alls.append(_dot(sums_ref[...], jnp.concatenate([hi, mid, lo], axis=0)))
        scores = [_dot_nt(q.astype(BF16), k.astype(BF16)) * masks_ref[levels]
                  for q, k in zip(qs, ks)]
        for l in range(levels):
            for u in range(U):
                e = jnp.exp(e_alls[u][(l + 1) * C:(l + 2) * C])
                scores[u] += _dot_nt((qs[u] * e).astype(BF16),
                                     (ks[u] * e).astype(BF16)) * masks_ref[l]
        parts = []
        for u in range(U):
            G = e_alls[u][0:C]
            G_last = G[C - 1:C]
            intra = _dot(scores[u].astype(BF16), vs[u])
            q_dec = (qs[u] * jnp.exp(G)).astype(BF16)
            k_dec = (ks[u] * jnp.exp(G_last - G)).astype(BF16)
            kv_t = _dot_tn(vs[u], k_dec)
            parts.append((q_dec, intra, kv_t, jnp.exp(G_last)))

        state_t = state_ref[...]
        for r0, (q_dec, intra, kv_t, decay_last) in zip(rows, parts):
            o = _dot_nt(q_dec, state_t.astype(BF16)) + intra
            gate = g_ref[pl.ds(r0, C), :].astype(F32)
            o_ref[pl.ds(r0, C), :] = (_rms_norm(o, nw_ref[...]) * _silu(gate)).astype(o_ref.dtype)
            state_t = state_t * decay_last + kv_t
        state_ref[...] = state_t
        return carry

    lax.fori_loop(0, S // (C * U), step, 0)


def _gla(proj, w_decay_up, b_decay, norm_w, *, C=GLA_CHUNK, U=GLA_CHUNKS_PER_STEP):
    B, S, _ = proj.shape
    H, dk, dv = GLA_HEADS, GLA_DK, GLA_DV
    assert S % (C * U) == 0
    sums_np, masks_np = _gla_tables(C)
    sums = jnp.asarray(sums_np, BF16)
    masks = jnp.asarray(masks_np, F32)
    kern = functools.partial(_gla_kernel, C=C, U=U)
    return pl.pallas_call(
        kern,
        grid=(B, H),
        in_specs=[
            pl.BlockSpec((None, S, dk), lambda b, h: (b, 0, COL_GQ // dk + h)),
            pl.BlockSpec((None, S, dk), lambda b, h: (b, 0, COL_GK // dk + h)),
            pl.BlockSpec((None, S, dv), lambda b, h: (b, 0, COL_GV // dv + h)),
            pl.BlockSpec((None, S, dv), lambda b, h: (b, 0, COL_GG // dv + h)),
            pl.BlockSpec((None, S, LR_PAD), lambda b, h: (b, 0, COL_LR // LR_PAD)),
            pl.BlockSpec((LR_PAD, dk), lambda b, h: (0, h)),
            pl.BlockSpec((1, dk), lambda b, h: (0, h)),
            pl.BlockSpec((1, dv), lambda b, h: (0, 0)),
            pl.BlockSpec(sums_np.shape, lambda b, h: (0, 0)),
            pl.BlockSpec(masks_np.shape, lambda b, h: (0, 0, 0)),
        ],
        out_specs=pl.BlockSpec((None, S, dv), lambda b, h: (b, 0, h)),
        out_shape=jax.ShapeDtypeStruct((B, S, H * dv), BF16),
        scratch_shapes=[pltpu.VMEM((S, dk), F32), pltpu.VMEM((dv, dk), F32)],
        compiler_params=pltpu.CompilerParams(
            dimension_semantics=("parallel", "parallel"),
            vmem_limit_bytes=VMEM_LIMIT_BYTES),
        name="gla",
    )(proj, proj, proj, proj, proj, w_decay_up, b_decay, norm_w, sums, masks)


def _moba_head(slope, q_ref, k_ref, v_ref, o_ref, vt_ref, ka_ref, qa_ref, t_ref, p_ref,
               *, BS, topk):
    S, hd = q_ref.shape
    NB = S // BS
    scale = hd ** -0.5
    scale2 = scale * LOG2E
    beta = slope * (1.0 / scale)
    neg = -1e30

    vt_ref[...] = v_ref[...].T

    lane = lax.broadcasted_iota(jnp.int32, (S, hd), 1)
    k_off = (lax.broadcasted_iota(jnp.int32, (S, hd), 0) & (BS - 1)).astype(F32)
    ka_ref[:, 0:hd] = k_ref[...]
    ka_ref[:, hd:] = jnp.where(lane < 3, k_off, 0.0).astype(BF16)
    b_hi, b_mid, b_lo = _split3_bf16(jnp.full((S, hd), beta, F32))
    qa_ref[:, 0:hd] = q_ref[...]
    qa_ref[:, hd:] = jnp.where(lane == 0, b_hi.astype(F32), jnp.where(
        lane == 1, b_mid.astype(F32), jnp.where(lane == 2, b_lo.astype(F32), 0.0))).astype(BF16)

    kmean = jnp.sum(k_ref[...].astype(F32).reshape(NB, BS, hd), axis=1) * (1.0 / BS)
    hi, mid, lo = _split3_bf16(kmean)
    kparts = jnp.concatenate([hi.astype(F32), mid.astype(F32), lo.astype(F32)], 0).astype(BF16)

    causal = (lax.broadcasted_iota(jnp.int32, (BS, BS), 0)
              <= lax.broadcasted_iota(jnp.int32, (BS, BS), 1))

    def base(n):
        return BS * (n * (n + 1) // 2)

    def pass1(n):
        q = q_ref[n * BS:(n + 1) * BS, :]
        sel = [None] * (n + 1)
        if n > topk:
            gp = _dot_nt(kparts, q)
            gate = [gp[j:j + 1] + gp[NB + j:NB + j + 1] + gp[2 * NB + j:2 * NB + j + 1]
                    for j in range(n)]
            for j in range(n):
                rank = jnp.zeros((1, BS), F32)
                for jj in range(n):
                    if jj == j:
                        continue
                    ahead = (gate[jj] >= gate[j]) if jj < j else (gate[jj] > gate[j])
                    rank += ahead.astype(F32)
                sel[j] = rank < float(topk)

        L = (n + 1) * BS
        t_ref[base(n):base(n) + L, :] = _dot_nt(ka_ref[0:L, :], qa_ref[n * BS:(n + 1) * BS, :])
        own = slice(base(n) + n * BS, base(n) + L)
        t_ref[own, :] = jnp.where(causal, t_ref[own, :], neg)
        offsets = [beta * float((n - j) * BS) for j in range(n + 1)]
        m = None
        for j in range(n + 1):
            t = t_ref[base(n) + j * BS:base(n) + (j + 1) * BS, :]
            mj = jnp.max(t, axis=0, keepdims=True) - offsets[j]
            if sel[j] is not None:
                mj = jnp.where(sel[j], mj, neg)
            m = mj if m is None else jnp.maximum(m, mj)
        return sel, offsets, m

    def pass2(n, sel, offsets, m):
        L = (n + 1) * BS
        l = jnp.zeros((1, BS), F32)
        for j in range(n + 1):
            rows = slice(base(n) + j * BS, base(n) + (j + 1) * BS)
            c = (m + offsets[j]) * scale2
            if sel[j] is not None:
                c = jnp.where(sel[j], c, -neg)
            p = jnp.exp2(t_ref[rows, :] * scale2 - c)
            l += jnp.sum(p, axis=0, keepdims=True)
            p_ref[rows, :] = p.astype(BF16)
        acc = _dot(vt_ref[:, 0:L], p_ref[base(n):base(n) + L, :])
        o_ref[n * BS:(n + 1) * BS, :] = (acc / l).T.astype(o_ref.dtype)

    return pass1, pass2


def _moba_kernel(slopes_ref, q_ref, k_ref, v_ref, o_ref, vt_ref, ka_ref, qa_ref, t_ref, p_ref,
                 *, BS, topk, heads):
    hd = q_ref.shape[1] // heads
    NB = q_ref.shape[0] // BS
    passes = []
    for hh in range(heads):
        cols = slice(hh * hd, (hh + 1) * hd)
        passes.append(_moba_head(
            slopes_ref[pl.program_id(1) * heads + hh], q_ref.at[:, cols], k_ref.at[:, cols],
            v_ref.at[:, cols], o_ref.at[:, cols], vt_ref.at[hh], ka_ref.at[hh], qa_ref.at[hh],
            t_ref.at[hh], p_ref.at[hh], BS=BS, topk=topk))
    stats = [pass1(0) for pass1, _ in passes]
    for n in range(NB):
        nxt = [pass1(n + 1) if n + 1 < NB else None for pass1, _ in passes]
        for (_, pass2), st in zip(passes, stats):
            pass2(n, *st)
        stats = nxt


def _moba(proj, slopes, *, heads=MOBA_HEADS_PER_STEP):
    B, S, _ = proj.shape
    H, hd, BS = MOBA_HEADS, MOBA_HEAD_DIM, MOBA_BLOCK
    assert S % BS == 0
    NB = S // BS
    tile_rows = BS * (NB * (NB + 1) // 2)
    kern = functools.partial(_moba_kernel, BS=BS, topk=MOBA_TOPK, heads=heads)
    w = heads * hd
    assert H % heads == 0 and COL_MQ % w == 0 and COL_MK % w == 0 and COL_MV % w == 0
    grid_spec = pltpu.PrefetchScalarGridSpec(
        num_scalar_prefetch=1,
        grid=(B, H // heads),
        in_specs=[
            pl.BlockSpec((None, S, w), lambda b, h, s: (b, 0, COL_MQ // w + h)),
            pl.BlockSpec((None, S, w), lambda b, h, s: (b, 0, COL_MK // w + h)),
            pl.BlockSpec((None, S, w), lambda b, h, s: (b, 0, COL_MV // w + h)),
        ],
        out_specs=pl.BlockSpec((None, S, w), lambda b, h, s: (b, 0, h)),
        scratch_shapes=[pltpu.VMEM((heads, hd, S), BF16), pltpu.VMEM((heads, S, 2 * hd), BF16),
                        pltpu.VMEM((heads, S, 2 * hd), BF16),
                        pltpu.VMEM((heads, tile_rows, BS), F32),
                        pltpu.VMEM((heads, tile_rows, BS), BF16)],
    )
    return pl.pallas_call(
        kern,
        grid_spec=grid_spec,
        out_shape=jax.ShapeDtypeStruct((B, S, H * hd), BF16),
        compiler_params=pltpu.CompilerParams(
            dimension_semantics=("parallel", "parallel"),
            vmem_limit_bytes=VMEM_LIMIT_BYTES),
        name="moba",
    )(slopes, proj, proj, proj)


def _out_proj_kernel(og_ref, om_ref, w_ref, h_ref, nw_ref, o_ref):
    kg = og_ref.shape[1]
    m = _dot(og_ref[...], w_ref[0:kg, :]) + _dot(om_ref[...], w_ref[kg:, :])
    o_ref[...] = h_ref[...] + _rms_norm(m, nw_ref[...])


def _out_proj(o_gla, o_moba, w, h, norm_w, *, tm):
    T, D = h.shape
    kg, km = o_gla.shape[1], o_moba.shape[1]
    assert T % tm == 0
    return pl.pallas_call(
        _out_proj_kernel,
        grid=(T // tm,),
        in_specs=[
            pl.BlockSpec((tm, kg), lambda i: (i, 0)),
            pl.BlockSpec((tm, km), lambda i: (i, 0)),
            pl.BlockSpec((kg + km, D), lambda i: (0, 0)),
            pl.BlockSpec((tm, D), lambda i: (i, 0)),
            pl.BlockSpec((1, D), lambda i: (0, 0)),
        ],
        out_specs=pl.BlockSpec((tm, D), lambda i: (i, 0)),
        out_shape=jax.ShapeDtypeStruct((T, D), F32),
        compiler_params=pltpu.CompilerParams(
            dimension_semantics=("parallel",),
            vmem_limit_bytes=VMEM_LIMIT_BYTES),
        name="out_proj",
    )(o_gla, o_moba, w, h, norm_w)


def _pack_w_in(w_in):
    D = w_in.shape[0]
    lr0 = 2 * GLA_QK_WIDTH + 2 * GLA_V_WIDTH
    lr1 = lr0 + GLA_GATE_RANK
    w = w_in.astype(BF16)
    return jnp.concatenate([
        w[:, :lr1],
        jnp.zeros((D, COL_MQ - COL_LR - GLA_GATE_RANK), BF16),
        w[:, lr1:],
        jnp.zeros((D, IN_PACKED_WIDTH - COL_END), BF16),
    ], axis=1)


def kernel(x, ffn1_pre_norm, ffn1_w_gate, ffn1_w_up, ffn1_w_down, ffn1_post_norm, mix_pre_norm, w_in, gla_w_decay_up, gla_b_decay, gla_out_norm, w_out, mix_post_norm, ffn2_pre_norm, ffn2_w_gate, ffn2_w_up, ffn2_w_down, ffn2_post_norm):
    B, S, D = x.shape
    T = B * S
    depth = w_in.shape[0]
    slopes = jnp.exp2(-ALIBI_MAX_BIAS * jnp.arange(1, MOBA_HEADS + 1, dtype=F32) / MOBA_HEADS)
    row = lambda a: a.reshape(1, -1).astype(F32)

    h = x.reshape(T, D)
    for l in range(depth):
        h = _ffn(h, row(ffn1_pre_norm[l]), ffn1_w_gate[l].astype(BF16), ffn1_w_up[l].astype(BF16),
                 ffn1_w_down[l].astype(BF16), row(ffn1_post_norm[l]), tm=FFN_TILE_M, tf=FFN_TILE_F)

        proj = _in_proj(h, row(mix_pre_norm[l]), _pack_w_in(w_in[l]), tm=1024, tn=IN_TILE_N)
        proj = proj.reshape(B, S, IN_PACKED_WIDTH)
        w_up = jnp.concatenate(
            [gla_w_decay_up[l], jnp.zeros((LR_PAD - GLA_GATE_RANK, GLA_QK_WIDTH), F32)], 0).astype(BF16)
        o_gla = _gla(proj, w_up, row(gla_b_decay[l]), row(gla_out_norm[l]))
        o_moba = _moba(proj, slopes)
        h = _out_proj(o_gla.reshape(T, GLA_V_WIDTH), o_moba.reshape(T, MOBA_WIDTH),
                      w_out[l].astype(BF16), h, row(mix_post_norm[l]), tm=512)

        h = _ffn(h, row(ffn2_pre_norm[l]), ffn2_w_gate[l].astype(BF16), ffn2_w_up[l].astype(BF16),
                 ffn2_w_down[l].astype(BF16), row(ffn2_post_norm[l]), tm=FFN_TILE_M, tf=FFN_TILE_F)
    return h.reshape(B, S, D)
```

```python
import functools
import math

import jax
import jax.numpy as jnp
import numpy as np
from jax import lax
from jax.experimental import pallas as pl
from jax.experimental.pallas import tpu as pltpu

F32 = jnp.float32
BF16 = jnp.bfloat16

NORM_EPS = 1e-6

GLA_HEADS = 4
GLA_DK = 128
GLA_DV = 256
GLA_CHUNK = 64
GLA_CHUNKS_PER_STEP = 32
FFN_TILE_M = 512
FFN_TILE_F = 512
GLA_GATE_RANK = 16
GLA_GATE_NORMALIZER = 16.0
MOBA_HEADS = 8
MOBA_HEAD_DIM = 128
MOBA_BLOCK = 256
MOBA_TOPK = 3
MOBA_HEADS_PER_STEP = 2
ALIBI_MAX_BIAS = 8.0

GLA_QK_WIDTH = GLA_HEADS * GLA_DK
GLA_V_WIDTH = GLA_HEADS * GLA_DV
MOBA_WIDTH = MOBA_HEADS * MOBA_HEAD_DIM

V7X_LANES = 128
V7X_VMEM_BYTES = 64 * 1024 * 1024
VMEM_LIMIT_BYTES = V7X_VMEM_BYTES - 8 * 1024 * 1024

LR_PAD = V7X_LANES
COL_GQ = 0
COL_GK = COL_GQ + GLA_QK_WIDTH
COL_GV = COL_GK + GLA_QK_WIDTH
COL_GG = COL_GV + GLA_V_WIDTH
COL_LR = COL_GG + GLA_V_WIDTH
COL_MQ = COL_LR + 2 * LR_PAD
COL_MK = COL_MQ + MOBA_WIDTH
COL_MV = COL_MK + MOBA_WIDTH
COL_END = COL_MV + MOBA_WIDTH
IN_TILE_M = 1024
IN_TILE_N = 1280
OUT_TILE_M = 512
IN_PACKED_WIDTH = -(-COL_END // IN_TILE_N) * IN_TILE_N

LOG2E = math.log2(math.e)

_NT = (((1,), (1,)), ((), ()))
_TN = (((0,), (0,)), ((), ()))


def _dot(a, b):
    return jnp.dot(a, b, preferred_element_type=F32)


def _dot_nt(a, b):
    return lax.dot_general(a, b, _NT, preferred_element_type=F32)


def _dot_tn(a, b):
    return lax.dot_general(a, b, _TN, preferred_element_type=F32)


def _rms_norm(x, w):
    ms = jnp.mean(x * x, axis=-1, keepdims=True)
    return x * lax.rsqrt(ms + NORM_EPS) * w


def _silu(x):
    return x / (1.0 + jnp.exp(-x))


def _split3_bf16(x):
    hi = x.astype(BF16)
    r1 = x - hi.astype(F32)
    mid = r1.astype(BF16)
    lo = (r1 - mid.astype(F32)).astype(BF16)
    return hi, mid, lo


def _ffn_kernel(x_ref, pre_ref, wg_ref, wu_ref, wd_ref, post_ref, o_ref, xn_ref):
    j = pl.program_id(1)

    @pl.when(j == 0)
    def _():
        xn_ref[...] = _rms_norm(x_ref[...], pre_ref[...]).astype(BF16)
        o_ref[...] = jnp.zeros_like(o_ref)

    xn = xn_ref[...]
    g = _dot(xn, wg_ref[...])
    u = _dot(xn, wu_ref[...])
    o_ref[...] += _dot((_silu(g) * u).astype(BF16), wd_ref[...])

    @pl.when(j == pl.num_programs(1) - 1)
    def _():
        o_ref[...] = x_ref[...] + 0.5 * _rms_norm(o_ref[...], post_ref[...])


def _ffn(x, pre_w, wg, wu, wd, post_w, *, tm, tf):
    T, D = x.shape
    F = wg.shape[1]
    assert T % tm == 0 and F % tf == 0
    return pl.pallas_call(
        _ffn_kernel,
        grid=(T // tm, F // tf),
        in_specs=[
            pl.BlockSpec((tm, D), lambda i, j: (i, 0)),
            pl.BlockSpec((1, D), lambda i, j: (0, 0)),
            pl.BlockSpec((D, tf), lambda i, j: (0, j)),
            pl.BlockSpec((D, tf), lambda i, j: (0, j)),
            pl.BlockSpec((tf, D), lambda i, j: (j, 0)),
            pl.BlockSpec((1, D), lambda i, j: (0, 0)),
        ],
        out_specs=pl.BlockSpec((tm, D), lambda i, j: (i, 0)),
        out_shape=jax.ShapeDtypeStruct((T, D), F32),
        scratch_shapes=[pltpu.VMEM((tm, D), BF16)],
        compiler_params=pltpu.CompilerParams(
            dimension_semantics=("parallel", "arbitrary"),
            vmem_limit_bytes=VMEM_LIMIT_BYTES),
        name="ffn",
    )(x, pre_w, wg, wu, wd, post_w)


def _in_proj_kernel(x_ref, nw_ref, w_ref, o_ref, xn_ref):
    @pl.when(pl.program_id(1) == 0)
    def _():
        xn_ref[...] = _rms_norm(x_ref[...], nw_ref[...]).astype(BF16)

    o_ref[...] = _dot(xn_ref[...], w_ref[...]).astype(o_ref.dtype)


def _in_proj(x, norm_w, w, *, tm, tn):
    T, D = x.shape
    N = w.shape[1]
    assert T % tm == 0 and N % tn == 0
    return pl.pallas_call(
        _in_proj_kernel,
        grid=(T // tm, N // tn),
        in_specs=[
            pl.BlockSpec((tm, D), lambda i, j: (i, 0)),
            pl.BlockSpec((1, D), lambda i, j: (0, 0)),
            pl.BlockSpec((D, tn), lambda i, j: (0, j)),
        ],
        out_specs=pl.BlockSpec((tm, tn), lambda i, j: (i, j)),
        out_shape=jax.ShapeDtypeStruct((T, N), BF16),
        scratch_shapes=[pltpu.VMEM((tm, D), BF16)],
        compiler_params=pltpu.CompilerParams(
            dimension_semantics=("parallel", "arbitrary"),
            vmem_limit_bytes=VMEM_LIMIT_BYTES),
        name="in_proj",
    )(x, norm_w, w)


def _gla_tables(C):
    levels = int(np.log2(C))
    idx = np.arange(C)
    t = idx[None, :]
    i = idx[:, None]
    sums = [np.tril(np.ones((C, C), np.float32))]
    masks = []
    for l in range(levels):
        b = 1 << l
        r = ((idx // (2 * b)) * (2 * b) + b - 1)[:, None]
        m = np.where((t > r) & (t <= i), 1.0, 0.0) + np.where((t > i) & (t <= r), 1.0, 0.0)
        sums.append(m.astype(np.float32))
        second = (idx // b) % 2 == 1
        same = (i // (2 * b)) == (t // (2 * b))
        masks.append((same & second[:, None] & ~second[None, :]).astype(np.float32))
    masks.append(np.eye(C, dtype=np.float32))
    sums = np.concatenate(sums, 0)
    return np.concatenate([sums, sums, sums], 1), np.stack(masks, 0)


def _gla_kernel(q_ref, k_ref, v_ref, g_ref, lr_ref, wd_ref, bd_ref, nw_ref, sums_ref,
                masks_ref, o_ref, la_ref, state_ref, *, C, U):
    S, dk = q_ref.shape
    levels = masks_ref.shape[0] - 1
    q_scale = dk ** -0.5

    z = _dot(lr_ref[...], wd_ref[...]) + bd_ref[...]
    log_sig = jnp.minimum(z, 0.0) - jnp.log(1.0 + jnp.exp(-jnp.abs(z)))
    la_ref[...] = log_sig * (1.0 / GLA_GATE_NORMALIZER)
    state_ref[...] = jnp.zeros_like(state_ref)

    def step(i, carry):
        base = i * (C * U)
        rows = [pl.multiple_of(base + u * C, C) for u in range(U)]
        qs = [q_ref[pl.ds(r0, C), :].astype(F32) * q_scale for r0 in rows]
        ks = [k_ref[pl.ds(r0, C), :].astype(F32) for r0 in rows]
        vs = [v_ref[pl.ds(r0, C), :] for r0 in rows]
        e_alls = []
        for r0 in rows:
            hi, mid, lo = _split3_bf16(la_ref[pl.ds(r0, C), :])
            e_alls.append(_dot(sums_ref[...], jnp.concatenate([hi, mid, lo], axis=0)))
        scores = [_dot_nt(q.astype(BF16), k.astype(BF16)) * masks_ref[levels]
                  for q, k in zip(qs, ks)]
        for l in range(levels):
            for u in range(U):
                e = jnp.exp(e_alls[u][(l + 1) * C:(l + 2) * C])
                scores[u] += _dot_nt((qs[u] * e).astype(BF16),
                                     (ks[u] * e).astype(BF16)) * masks_ref[l]
        parts = []
        for u in range(U):
            G = e_alls[u][0:C]
            G_last = G[C - 1:C]
            intra = _dot(scores[u].astype(BF16), vs[u])
            q_dec = (qs[u] * jnp.exp(G)).astype(BF16)
            k_dec = (ks[u] * jnp.exp(G_last - G)).astype(BF16)
            kv_t = _dot_tn(vs[u], k_dec)
            parts.append((q_dec, intra, kv_t, jnp.exp(G_last)))

        state_t = state_ref[...]
        for r0, (q_dec, intra, kv_t, decay_last) in zip(rows, parts):
            o = _dot_nt(q_dec, state_t.astype(BF16)) + intra
            gate = g_ref[pl.ds(r0, C), :].astype(F32)
            o_ref[pl.ds(r0, C), :] = (_rms_norm(o, nw_ref[...]) * _silu(gate)).astype(o_ref.dtype)
            state_t = state_t * decay_last + kv_t
        state_ref[...] = state_t
        return carry

    lax.fori_loop(0, S // (C * U), step, 0)


def _gla(proj, w_decay_up, b_decay, norm_w, *, C=GLA_CHUNK, U=GLA_CHUNKS_PER_STEP):
    B, S, _ = proj.shape
    H, dk, dv = GLA_HEADS, GLA_DK, GLA_DV
    assert S % (C * U) == 0
    sums_np, masks_np = _gla_tables(C)
    sums = jnp.asarray(sums_np, BF16)
    masks = jnp.asarray(masks_np, F32)
    kern = functools.partial(_gla_kernel, C=C, U=U)
    return pl.pallas_call(
        kern,
        grid=(B, H),
        in_specs=[
            pl.BlockSpec((None, S, dk), lambda b, h: (b, 0, COL_GQ // dk + h)),
            pl.BlockSpec((None, S, dk), lambda b, h: (b, 0, COL_GK // dk + h)),
            pl.BlockSpec((None, S, dv), lambda b, h: (b, 0, COL_GV // dv + h)),
            pl.BlockSpec((None, S, dv), lambda b, h: (b, 0, COL_GG // dv + h)),
            pl.BlockSpec((None, S, LR_PAD), lambda b, h: (b, 0, COL_LR // LR_PAD)),
            pl.BlockSpec((LR_PAD, dk), lambda b, h: (0, h)),
            pl.BlockSpec((1, dk), lambda b, h: (0, h)),
            pl.BlockSpec((1, dv), lambda b, h: (0, 0)),
            pl.BlockSpec(sums_np.shape, lambda b, h: (0, 0)),
            pl.BlockSpec(masks_np.shape, lambda b, h: (0, 0, 0)),
        ],
        out_specs=pl.BlockSpec((None, S, dv), lambda b, h: (b, 0, h)),
        out_shape=jax.ShapeDtypeStruct((B, S, H * dv), BF16),
        scratch_shapes=[pltpu.VMEM((S, dk), F32), pltpu.VMEM((dv, dk), F32)],
        compiler_params=pltpu.CompilerParams(
            dimension_semantics=("parallel", "parallel"),
            vmem_limit_bytes=VMEM_LIMIT_BYTES),
        name="gla",
    )(proj, proj, proj, proj, proj, w_decay_up, b_decay, norm_w, sums, masks)


def _moba_head(slope, q_ref, k_ref, v_ref, o_ref, vt_ref, ka_ref, qa_ref, t_ref, p_ref,
               *, BS, topk):
    S, hd = q_ref.shape
    NB = S // BS
    scale = hd ** -0.5
    scale2 = scale * LOG2E
    beta = slope * (1.0 / scale)
    neg = -1e30

    vt_ref[...] = v_ref[...].T

    lane = lax.broadcasted_iota(jnp.int32, (S, hd), 1)
    k_off = (lax.broadcasted_iota(jnp.int32, (S, hd), 0) & (BS - 1)).astype(F32)
    ka_ref[:, 0:hd] = k_ref[...]
    ka_ref[:, hd:] = jnp.where(lane < 3, k_off, 0.0).astype(BF16)
    b_hi, b_mid, b_lo = _split3_bf16(jnp.full((hd, S), beta, F32))
    row = lax.broadcasted_iota(jnp.int32, (hd, S), 0)
    qa_ref[0:hd, :] = q_ref[...].T
    qa_ref[hd:, :] = jnp.where(row == 0, b_hi.astype(F32), jnp.where(
        row == 1, b_mid.astype(F32), jnp.where(row == 2, b_lo.astype(F32), 0.0))).astype(BF16)

    kmean = jnp.sum(k_ref[...].astype(F32).reshape(NB, BS, hd), axis=1) * (1.0 / BS)
    hi, mid, lo = _split3_bf16(kmean)
    kparts = jnp.concatenate([hi.astype(F32), mid.astype(F32), lo.astype(F32)], 0).astype(BF16)

    causal = (lax.broadcasted_iota(jnp.int32, (BS, BS), 0)
              <= lax.broadcasted_iota(jnp.int32, (BS, BS), 1))

    def base(n):
        return BS * (n * (n + 1) // 2)

    def pass1(n):
        cols = slice(n * BS, (n + 1) * BS)
        sel = [None] * (n + 1)
        if n > topk:
            gp = _dot(kparts, qa_ref[0:hd, cols])
            gate = [gp[j:j + 1] + gp[NB + j:NB + j + 1] + gp[2 * NB + j:2 * NB + j + 1]
                    for j in range(n)]
            for j in range(n):
                rank = jnp.zeros((1, BS), F32)
                for jj in range(n):
                    if jj == j:
                        continue
                    ahead = (gate[jj] >= gate[j]) if jj < j else (gate[jj] > gate[j])
                    rank += ahead.astype(F32)
                sel[j] = rank < float(topk)

        L = (n + 1) * BS
        t_ref[base(n):base(n) + L, :] = _dot(ka_ref[0:L, :], qa_ref[:, cols])
        own = slice(base(n) + n * BS, base(n) + L)
        t_ref[own, :] = jnp.where(causal, t_ref[own, :], neg)
        offsets = [beta * float((n - j) * BS) for j in range(n + 1)]
        m = None
        for j in range(n + 1):
            t = t_ref[base(n) + j * BS:base(n) + (j + 1) * BS, :]
            mj = jnp.max(t, axis=0, keepdims=True) - offsets[j]
            if sel[j] is not None:
                mj = jnp.where(sel[j], mj, neg)
            m = mj if m is None else jnp.maximum(m, mj)
        return sel, offsets, m

    def pass2(n, sel, offsets, m):
        L = (n + 1) * BS
        l = jnp.zeros((1, BS), F32)
        for j in range(n + 1):
            rows = slice(base(n) + j * BS, base(n) + (j + 1) * BS)
            c = (m + offsets[j]) * scale2
            if sel[j] is not None:
                c = jnp.where(sel[j], c, -neg)
            p = jnp.exp2(t_ref[rows, :] * scale2 - c)
            l += jnp.sum(p, axis=0, keepdims=True)
            p_ref[rows, :] = p.astype(BF16)
        acc = _dot(vt_ref[:, 0:L], p_ref[base(n):base(n) + L, :])
        o_ref[n * BS:(n + 1) * BS, :] = (acc / l).T.astype(o_ref.dtype)

    return pass1, pass2


def _moba_kernel(slopes_ref, q_ref, k_ref, v_ref, o_ref, vt_ref, ka_ref, qa_ref, t_ref, p_ref,
                 *, BS, topk, heads):
    hd = q_ref.shape[1] // heads
    NB = q_ref.shape[0] // BS
    passes = []
    for hh in range(heads):
        cols = slice(hh * hd, (hh + 1) * hd)
        passes.append(_moba_head(
            slopes_ref[pl.program_id(1) * heads + hh], q_ref.at[:, cols], k_ref.at[:, cols],
            v_ref.at[:, cols], o_ref.at[:, cols], vt_ref.at[hh], ka_ref.at[hh], qa_ref.at[hh],
            t_ref.at[hh], p_ref.at[hh], BS=BS, topk=topk))
    stats = [pass1(0) for pass1, _ in passes]
    for n in range(NB):
        nxt = [pass1(n + 1) if n + 1 < NB else None for pass1, _ in passes]
        for (_, pass2), st in zip(passes, stats):
            pass2(n, *st)
        stats = nxt


def _moba(proj, slopes, *, heads=MOBA_HEADS_PER_STEP):
    B, S, _ = proj.shape
    H, hd, BS = MOBA_HEADS, MOBA_HEAD_DIM, MOBA_BLOCK
    assert S % BS == 0
    NB = S // BS
    tile_rows = BS * (NB * (NB + 1) // 2)
    kern = functools.partial(_moba_kernel, BS=BS, topk=MOBA_TOPK, heads=heads)
    w = heads * hd
    assert H % heads == 0 and COL_MQ % w == 0 and COL_MK % w == 0 and COL_MV % w == 0
    grid_spec = pltpu.PrefetchScalarGridSpec(
        num_scalar_prefetch=1,
        grid=(B, H // heads),
        in_specs=[
            pl.BlockSpec((None, S, w), lambda b, h, s: (b, 0, COL_MQ // w + h)),
            pl.BlockSpec((None, S, w), lambda b, h, s: (b, 0, COL_MK // w + h)),
            pl.BlockSpec((None, S, w), lambda b, h, s: (b, 0, COL_MV // w + h)),
        ],
        out_specs=pl.BlockSpec((None, S, w), lambda b, h, s: (b, 0, h)),
        scratch_shapes=[pltpu.VMEM((heads, hd, S), BF16), pltpu.VMEM((heads, S, 2 * hd), BF16),
                        pltpu.VMEM((heads, 2 * hd, S), BF16),
                        pltpu.VMEM((heads, tile_rows, BS), F32),
                        pltpu.VMEM((heads, tile_rows, BS), BF16)],
    )
    return pl.pallas_call(
        kern,
        grid_spec=grid_spec,
        out_shape=jax.ShapeDtypeStruct((B, S, H * hd), BF16),
        compiler_params=pltpu.CompilerParams(
            dimension_semantics=("parallel", "parallel"),
            vmem_limit_bytes=VMEM_LIMIT_BYTES),
        name="moba",
    )(slopes, proj, proj, proj)


def _out_proj_kernel(og_ref, om_ref, w_ref, h_ref, nw_ref, o_ref):
    kg = og_ref.shape[1]
    m = _dot(og_ref[...], w_ref[0:kg, :]) + _dot(om_ref[...], w_ref[kg:, :])
    o_ref[...] = h_ref[...] + _rms_norm(m, nw_ref[...])


def _out_proj(o_gla, o_moba, w, h, norm_w, *, tm):
    T, D = h.shape
    kg, km = o_gla.shape[1], o_moba.shape[1]
    assert T % tm == 0
    return pl.pallas_call(
        _out_proj_kernel,
        grid=(T // tm,),
        in_specs=[
            pl.BlockSpec((tm, kg), lambda i: (i, 0)),
            pl.BlockSpec((tm, km), lambda i: (i, 0)),
            pl.BlockSpec((kg + km, D), lambda i: (0, 0)),
            pl.BlockSpec((tm, D), lambda i: (i, 0)),
            pl.BlockSpec((1, D), lambda i: (0, 0)),
        ],
        out_specs=pl.BlockSpec((tm, D), lambda i: (i, 0)),
        out_shape=jax.ShapeDtypeStruct((T, D), F32),
        compiler_params=pltpu.CompilerParams(
            dimension_semantics=("parallel",),
            vmem_limit_bytes=VMEM_LIMIT_BYTES),
        name="out_proj",
    )(o_gla, o_moba, w, h, norm_w)


def _pack_w_in(w_in):
    D = w_in.shape[0]
    lr0 = 2 * GLA_QK_WIDTH + 2 * GLA_V_WIDTH
    lr1 = lr0 + GLA_GATE_RANK
    w = w_in.astype(BF16)
    return jnp.concatenate([
        w[:, :lr1],
        jnp.zeros((D, COL_MQ - COL_LR - GLA_GATE_RANK), BF16),
        w[:, lr1:],
        jnp.zeros((D, IN_PACKED_WIDTH - COL_END), BF16),
    ], axis=1)


def kernel(x, ffn1_pre_norm, ffn1_w_gate, ffn1_w_up, ffn1_w_down, ffn1_post_norm, mix_pre_norm, w_in, gla_w_decay_up, gla_b_decay, gla_out_norm, w_out, mix_post_norm, ffn2_pre_norm, ffn2_w_gate, ffn2_w_up, ffn2_w_down, ffn2_post_norm):
    B, S, D = x.shape
    T = B * S
    depth = w_in.shape[0]
    slopes = jnp.exp2(-ALIBI_MAX_BIAS * jnp.arange(1, MOBA_HEADS + 1, dtype=F32) / MOBA_HEADS)
    row = lambda a: a.reshape(1, -1).astype(F32)

    h = x.reshape(T, D)
    for l in range(depth):
        h = _ffn(h, row(ffn1_pre_norm[l]), ffn1_w_gate[l].astype(BF16), ffn1_w_up[l].astype(BF16),
                 ffn1_w_down[l].astype(BF16), row(ffn1_post_norm[l]), tm=FFN_TILE_M, tf=FFN_TILE_F)

        proj = _in_proj(h, row(mix_pre_norm[l]), _pack_w_in(w_in[l]), tm=IN_TILE_M, tn=IN_TILE_N)
        proj = proj.reshape(B, S, IN_PACKED_WIDTH)
        w_up = jnp.concatenate(
            [gla_w_decay_up[l], jnp.zeros((LR_PAD - GLA_GATE_RANK, GLA_QK_WIDTH), F32)], 0).astype(BF16)
        o_gla = _gla(proj, w_up, row(gla_b_decay[l]), row(gla_out_norm[l]))
        o_moba = _moba(proj, slopes)
        h = _out_proj(o_gla.reshape(T, GLA_V_WIDTH), o_moba.reshape(T, MOBA_WIDTH),
                      w_out[l].astype(BF16), h, row(mix_post_norm[l]), tm=OUT_TILE_M)

        h = _ffn(h, row(ffn2_pre_norm[l]), ffn2_w_gate[l].astype(BF16), ffn2_w_up[l].astype(BF16),
                 ffn2_w_down[l].astype(BF16), row(ffn2_post_norm[l]), tm=FFN_TILE_M, tf=FFN_TILE_F)
    return h.reshape(B, S, D)
```

```python
import functools
import math

import jax
import jax.numpy as jnp
import numpy as np
from jax import lax
from jax.experimental import pallas as pl
from jax.experimental.pallas import tpu as pltpu

F32 = jnp.float32
BF16 = jnp.bfloat16

NORM_EPS = 1e-6

GLA_HEADS = 4
GLA_DK = 128
GLA_DV = 256
GLA_CHUNK = 64
GLA_CHUNKS_PER_STEP = 32
FFN_TILE_M = 512
FFN_TILE_F = 512
GLA_GATE_RANK = 16
GLA_GATE_NORMALIZER = 16.0
MOBA_HEADS = 8
MOBA_HEAD_DIM = 128
MOBA_BLOCK = 256
MOBA_TOPK = 3
MOBA_HEADS_PER_STEP = 2
ALIBI_MAX_BIAS = 8.0

GLA_QK_WIDTH = GLA_HEADS * GLA_DK
GLA_V_WIDTH = GLA_HEADS * GLA_DV
MOBA_WIDTH = MOBA_HEADS * MOBA_HEAD_DIM

V7X_LANES = 128
V7X_VMEM_BYTES = 64 * 1024 * 1024
VMEM_LIMIT_BYTES = V7X_VMEM_BYTES - 8 * 1024 * 1024

LR_PAD = V7X_LANES
COL_GQ = 0
COL_GK = COL_GQ + GLA_QK_WIDTH
COL_GV = COL_GK + GLA_QK_WIDTH
COL_GG = COL_GV + GLA_V_WIDTH
COL_LR = COL_GG + GLA_V_WIDTH
COL_MQ = COL_LR + 2 * LR_PAD
COL_MK = COL_MQ + MOBA_WIDTH
COL_MV = COL_MK + MOBA_WIDTH
COL_END = COL_MV + MOBA_WIDTH
IN_TILE_M = 1024
IN_TILE_N = 1280
OUT_TILE_M = 512
IN_PACKED_WIDTH = -(-COL_END // IN_TILE_N) * IN_TILE_N

LOG2E = math.log2(math.e)

_NT = (((1,), (1,)), ((), ()))
_TN = (((0,), (0,)), ((), ()))


def _dot(a, b):
    return jnp.dot(a, b, preferred_element_type=F32)


def _dot_nt(a, b):
    return lax.dot_general(a, b, _NT, preferred_element_type=F32)


def _dot_tn(a, b):
    return lax.dot_general(a, b, _TN, preferred_element_type=F32)


def _rms_norm(x, w):
    ms = jnp.mean(x * x, axis=-1, keepdims=True)
    return x * lax.rsqrt(ms + NORM_EPS) * w


def _silu(x):
    return x / (1.0 + jnp.exp(-x))


def _split3_bf16(x):
    hi = x.astype(BF16)
    r1 = x - hi.astype(F32)
    mid = r1.astype(BF16)
    lo = (r1 - mid.astype(F32)).astype(BF16)
    return hi, mid, lo


def _ffn_kernel(x_ref, pre_ref, wg_ref, wu_ref, wd_ref, post_ref, o_ref, xn_ref):
    j = pl.program_id(1)

    @pl.when(j == 0)
    def _():
        xn_ref[...] = _rms_norm(x_ref[...], pre_ref[...]).astype(BF16)
        o_ref[...] = jnp.zeros_like(o_ref)

    xn = xn_ref[...]
    g = _dot(xn, wg_ref[...])
    u = _dot(xn, wu_ref[...])
    o_ref[...] += _dot((_silu(g) * u).astype(BF16), wd_ref[...])

    @pl.when(j == pl.num_programs(1) - 1)
    def _():
        o_ref[...] = x_ref[...] + 0.5 * _rms_norm(o_ref[...], post_ref[...])


def _ffn(x, pre_w, wg, wu, wd, post_w, *, tm, tf):
    T, D = x.shape
    F = wg.shape[1]
    assert T % tm == 0 and F % tf == 0
    return pl.pallas_call(
        _ffn_kernel,
        grid=(T // tm, F // tf),
        in_specs=[
            pl.BlockSpec((tm, D), lambda i, j: (i, 0)),
            pl.BlockSpec((1, D), lambda i, j: (0, 0)),
            pl.BlockSpec((D, tf), lambda i, j: (0, j)),
            pl.BlockSpec((D, tf), lambda i, j: (0, j)),
            pl.BlockSpec((tf, D), lambda i, j: (j, 0)),
            pl.BlockSpec((1, D), lambda i, j: (0, 0)),
        ],
        out_specs=pl.BlockSpec((tm, D), lambda i, j: (i, 0)),
        out_shape=jax.ShapeDtypeStruct((T, D), F32),
        scratch_shapes=[pltpu.VMEM((tm, D), BF16)],
        compiler_params=pltpu.CompilerParams(
            dimension_semantics=("parallel", "arbitrary"),
            vmem_limit_bytes=VMEM_LIMIT_BYTES),
        name="ffn",
    )(x, pre_w, wg, wu, wd, post_w)


def _in_proj_kernel(x_ref, nw_ref, w_ref, o_ref, xn_ref):
    @pl.when(pl.program_id(1) == 0)
    def _():
        xn_ref[...] = _rms_norm(x_ref[...], nw_ref[...]).astype(BF16)

    o_ref[...] = _dot(xn_ref[...], w_ref[...]).astype(o_ref.dtype)


def _in_proj(x, norm_w, w, *, tm, tn):
    T, D = x.shape
    N = w.shape[1]
    assert T % tm == 0 and N % tn == 0
    return pl.pallas_call(
        _in_proj_kernel,
        grid=(T // tm, N // tn),
        in_specs=[
            pl.BlockSpec((tm, D), lambda i, j: (i, 0)),
            pl.BlockSpec((1, D), lambda i, j: (0, 0)),
            pl.BlockSpec((D, tn), lambda i, j: (0, j)),
        ],
        out_specs=pl.BlockSpec((tm, tn), lambda i, j: (i, j)),
        out_shape=jax.ShapeDtypeStruct((T, N), BF16),
        scratch_shapes=[pltpu.VMEM((tm, D), BF16)],
        compiler_params=pltpu.CompilerParams(
            dimension_semantics=("parallel", "arbitrary"),
            vmem_limit_bytes=VMEM_LIMIT_BYTES),
        name="in_proj",
    )(x, norm_w, w)


def _gla_tables(C):
    levels = int(np.log2(C))
    idx = np.arange(C)
    t = idx[None, :]
    i = idx[:, None]
    sums = [np.tril(np.ones((C, C), np.float32))]
    masks = []
    for l in range(levels):
        b = 1 << l
        r = ((idx // (2 * b)) * (2 * b) + b - 1)[:, None]
        m = np.where((t > r) & (t <= i), 1.0, 0.0) + np.where((t > i) & (t <= r), 1.0, 0.0)
        sums.append(m.astype(np.float32))
        second = (idx // b) % 2 == 1
        same = (i // (2 * b)) == (t // (2 * b))
        masks.append((same & second[:, None] & ~second[None, :]).astype(np.float32))
    masks.append(np.eye(C, dtype=np.float32))
    sums = np.concatenate(sums, 0)
    return np.concatenate([sums, sums, sums], 1), np.stack(masks, 0)


def _gla_kernel(q_ref, k_ref, v_ref, g_ref, lr_ref, wd_ref, bd_ref, nw_ref, sums_ref,
                masks_ref, o_ref, la_ref, state_ref, *, C, U):
    S, dk = q_ref.shape
    levels = masks_ref.shape[0] - 1
    q_scale = dk ** -0.5

    z = _dot(lr_ref[...], wd_ref[...]) + bd_ref[...]
    log_sig = jnp.minimum(z, 0.0) - jnp.log(1.0 + jnp.exp(-jnp.abs(z)))
    la_ref[...] = log_sig * (1.0 / GLA_GATE_NORMALIZER)
    state_ref[...] = jnp.zeros_like(state_ref)

    def step(i, carry):
        base = i * (C * U)
        rows = [pl.multiple_of(base + u * C, C) for u in range(U)]
        qs = [q_ref[pl.ds(r0, C), :].astype(F32) * q_scale for r0 in rows]
        ks = [k_ref[pl.ds(r0, C), :].astype(F32) for r0 in rows]
        vs = [v_ref[pl.ds(r0, C), :] for r0 in rows]
        e_alls = []
        for r0 in rows:
            hi, mid, lo = _split3_bf16(la_ref[pl.ds(r0, C), :])
            e_alls.append(_dot(sums_ref[...], jnp.concatenate([hi, mid, lo], axis=0)))
        scores = [_dot_nt(q.astype(BF16), k.astype(BF16)) * masks_ref[levels]
                  for q, k in zip(qs, ks)]
        for l in range(levels):
            for u in range(U):
                e = jnp.exp(e_alls[u][(l + 1) * C:(l + 2) * C])
                scores[u] += _dot_nt((qs[u] * e).astype(BF16),
                                     (ks[u] * e).astype(BF16)) * masks_ref[l]
        parts = []
        for u in range(U):
            G = e_alls[u][0:C]
            G_last = G[C - 1:C]
            intra = _dot(scores[u].astype(BF16), vs[u])
            q_dec = (qs[u] * jnp.exp(G)).astype(BF16)
            k_dec = (ks[u] * jnp.exp(G_last - G)).astype(BF16)
            kv_t = _dot_tn(vs[u], k_dec)
            parts.append((q_dec, intra, kv_t, jnp.exp(G_last)))

        state_t = state_ref[...]
        for r0, (q_dec, intra, kv_t, decay_last) in zip(rows, parts):
            o = _dot_nt(q_dec, state_t.astype(BF16)) + intra
            gate = g_ref[pl.ds(r0, C), :].astype(F32)
            o_ref[pl.ds(r0, C), :] = (_rms_norm(o, nw_ref[...]) * _silu(gate)).astype(o_ref.dtype)
            state_t = state_t * decay_last + kv_t
        state_ref[...] = state_t
        return carry

    lax.fori_loop(0, S // (C * U), step, 0)


def _gla(proj, w_decay_up, b_decay, norm_w, *, C=GLA_CHUNK, U=GLA_CHUNKS_PER_STEP):
    B, S, _ = proj.shape
    H, dk, dv = GLA_HEADS, GLA_DK, GLA_DV
    assert S % (C * U) == 0
    sums_np, masks_np = _gla_tables(C)
    sums = jnp.asarray(sums_np, BF16)
    masks = jnp.asarray(masks_np, F32)
    kern = functools.partial(_gla_kernel, C=C, U=U)
    return pl.pallas_call(
        kern,
        grid=(B, H),
        in_specs=[
            pl.BlockSpec((None, S, dk), lambda b, h: (b, 0, COL_GQ // dk + h)),
            pl.BlockSpec((None, S, dk), lambda b, h: (b, 0, COL_GK // dk + h)),
            pl.BlockSpec((None, S, dv), lambda b, h: (b, 0, COL_GV // dv + h)),
            pl.BlockSpec((None, S, dv), lambda b, h: (b, 0, COL_GG // dv + h)),
            pl.BlockSpec((None, S, LR_PAD), lambda b, h: (b, 0, COL_LR // LR_PAD)),
            pl.BlockSpec((LR_PAD, dk), lambda b, h: (0, h)),
            pl.BlockSpec((1, dk), lambda b, h: (0, h)),
            pl.BlockSpec((1, dv), lambda b, h: (0, 0)),
            pl.BlockSpec(sums_np.shape, lambda b, h: (0, 0)),
            pl.BlockSpec(masks_np.shape, lambda b, h: (0, 0, 0)),
        ],
        out_specs=pl.BlockSpec((None, S, dv), lambda b, h: (b, 0, h)),
        out_shape=jax.ShapeDtypeStruct((B, S, H * dv), BF16),
        scratch_shapes=[pltpu.VMEM((S, dk), F32), pltpu.VMEM((dv, dk), F32)],
        compiler_params=pltpu.CompilerParams(
            dimension_semantics=("parallel", "parallel"),
            vmem_limit_bytes=VMEM_LIMIT_BYTES),
        name="gla",
    )(proj, proj, proj, proj, proj, w_decay_up, b_decay, norm_w, sums, masks)


def _moba_head(slope, q_ref, k_ref, v_ref, o_ref, vt_ref, ka_ref, qa_ref, t_ref, p_ref,
               *, BS, topk):
    S, hd = q_ref.shape
    NB = S // BS
    scale = hd ** -0.5
    scale2 = scale * LOG2E
    beta = slope * (1.0 / scale)
    neg = -1e30

    vt_ref[...] = v_ref[...].T

    lane = lax.broadcasted_iota(jnp.int32, (S, hd), 1)
    k_off = (lax.broadcasted_iota(jnp.int32, (S, hd), 0) & (BS - 1)).astype(F32)
    ka_ref[:, 0:hd] = k_ref[...]
    ka_ref[:, hd:] = jnp.where(lane < 3, k_off, 0.0).astype(BF16)
    b_hi, b_mid, b_lo = _split3_bf16(jnp.full((hd, S), beta, F32))
    row = lax.broadcasted_iota(jnp.int32, (hd, S), 0)
    qa_ref[0:hd, :] = q_ref[...].T
    qa_ref[hd:, :] = jnp.where(row == 0, b_hi.astype(F32), jnp.where(
        row == 1, b_mid.astype(F32), jnp.where(row == 2, b_lo.astype(F32), 0.0))).astype(BF16)

    kmean = jnp.sum(k_ref[...].astype(F32).reshape(NB, BS, hd), axis=1) * (1.0 / BS)
    hi, mid, lo = _split3_bf16(kmean)
    kparts = jnp.concatenate([hi.astype(F32), mid.astype(F32), lo.astype(F32)], 0).astype(BF16)

    causal = (lax.broadcasted_iota(jnp.int32, (BS, BS), 0)
              <= lax.broadcasted_iota(jnp.int32, (BS, BS), 1))

    def base(n):
        return BS * (n * (n + 1) // 2)

    def pass1(n):
        cols = slice(n * BS, (n + 1) * BS)
        sel = [None] * (n + 1)
        if n > topk:
            gp = _dot(kparts, qa_ref[0:hd, cols])
            gate = [gp[j:j + 1] + gp[NB + j:NB + j + 1] + gp[2 * NB + j:2 * NB + j + 1]
                    for j in range(n)]
            for j in range(n):
                rank = jnp.zeros((1, BS), F32)
                for jj in range(n):
                    if jj == j:
                        continue
                    ahead = (gate[jj] >= gate[j]) if jj < j else (gate[jj] > gate[j])
                    rank += ahead.astype(F32)
                sel[j] = rank < float(topk)

        L = (n + 1) * BS
        t_ref[base(n):base(n) + L, :] = _dot(ka_ref[0:L, :], qa_ref[:, cols])
        own = slice(base(n) + n * BS, base(n) + L)
        t_ref[own, :] = jnp.where(causal, t_ref[own, :], neg)
        offsets = [beta * float((n - j) * BS) for j in range(n + 1)]
        m = None
        for j in range(n + 1):
            t = t_ref[base(n) + j * BS:base(n) + (j + 1) * BS, :]
            mj = jnp.max(t, axis=0, keepdims=True) - offsets[j]
            if sel[j] is not None:
                mj = jnp.where(sel[j], mj, neg)
            m = mj if m is None else jnp.maximum(m, mj)
        return sel, offsets, m

    def pass2(n, sel, offsets, m):
        L = (n + 1) * BS
        l = jnp.zeros((1, BS), F32)
        for j in range(n + 1):
            rows = slice(base(n) + j * BS, base(n) + (j + 1) * BS)
            c = (m + offsets[j]) * scale2
            if sel[j] is not None:
                c = jnp.where(sel[j], c, -neg)
            p = jnp.exp2(t_ref[rows, :] * scale2 - c)
            l += jnp.sum(p, axis=0, keepdims=True)
            p_ref[rows, :] = p.astype(BF16)
        acc = _dot(vt_ref[:, 0:L], p_ref[base(n):base(n) + L, :])
        o_ref[n * BS:(n + 1) * BS, :] = (acc / l).T.astype(o_ref.dtype)

    return pass1, pass2


def _moba_kernel(slopes_ref, q_ref, k_ref, v_ref, o_ref, vt_ref, ka_ref, qa_ref, t_ref, p_ref,
                 *, BS, topk, heads):
    hd = q_ref.shape[1] // heads
    NB = q_ref.shape[0] // BS
    passes = []
    for hh in range(heads):
        cols = slice(hh * hd, (hh + 1) * hd)
        passes.append(_moba_head(
            slopes_ref[pl.program_id(1) * heads + hh], q_ref.at[:, cols], k_ref.at[:, cols],
            v_ref.at[:, cols], o_ref.at[:, cols], vt_ref.at[hh], ka_ref.at[hh], qa_ref.at[hh],
            t_ref.at[hh], p_ref.at[hh], BS=BS, topk=topk))
    stats = [pass1(0) for pass1, _ in passes]
    for n in range(NB):
        nxt = [pass1(n + 1) if n + 1 < NB else None for pass1, _ in passes]
        for (_, pass2), st in zip(passes, stats):
            pass2(n, *st)
        stats = nxt


def _moba(proj, slopes, *, heads=MOBA_HEADS_PER_STEP):
    B, S, _ = proj.shape
    H, hd, BS = MOBA_HEADS, MOBA_HEAD_DIM, MOBA_BLOCK
    assert S % BS == 0
    NB = S // BS
    tile_rows = BS * (NB * (NB + 1) // 2)
    kern = functools.partial(_moba_kernel, BS=BS, topk=MOBA_TOPK, heads=heads)
    w = heads * hd
    assert H % heads == 0 and COL_MQ % w == 0 and COL_MK % w == 0 and COL_MV % w == 0
    grid_spec = pltpu.PrefetchScalarGridSpec(
        num_scalar_prefetch=1,
        grid=(B, H // heads),
        in_specs=[
            pl.BlockSpec((None, S, w), lambda b, h, s: (b, 0, COL_MQ // w + h)),
            pl.BlockSpec((None, S, w), lambda b, h, s: (b, 0, COL_MK // w + h)),
            pl.BlockSpec((None, S, w), lambda b, h, s: (b, 0, COL_MV // w + h)),
        ],
        out_specs=pl.BlockSpec((None, S, w), lambda b, h, s: (b, 0, h)),
        scratch_shapes=[pltpu.VMEM((heads, hd, S), BF16), pltpu.VMEM((heads, S, 2 * hd), BF16),
                        pltpu.VMEM((heads, 2 * hd, S), BF16),
                        pltpu.VMEM((heads, tile_rows, BS), F32),
                        pltpu.VMEM((heads, tile_rows, BS), BF16)],
    )
    return pl.pallas_call(
        kern,
        grid_spec=grid_spec,
        out_shape=jax.ShapeDtypeStruct((B, S, H * hd), BF16),
        compiler_params=pltpu.CompilerParams(
            dimension_semantics=("parallel", "parallel"),
            vmem_limit_bytes=VMEM_LIMIT_BYTES),
        name="moba",
    )(slopes, proj, proj, proj)


def _out_proj_kernel(og_ref, om_ref, w_ref, h_ref, nw_ref, o_ref):
    kg = og_ref.shape[1]
    m = _dot(og_ref[...], w_ref[0:kg, :]) + _dot(om_ref[...], w_ref[kg:, :])
    o_ref[...] = h_ref[...] + _rms_norm(m, nw_ref[...])


def _out_proj(o_gla, o_moba, w, h, norm_w, *, tm):
    T, D = h.shape
    kg, km = o_gla.shape[1], o_moba.shape[1]
    assert T % tm == 0
    return pl.pallas_call(
        _out_proj_kernel,
        grid=(T // tm,),
        in_specs=[
            pl.BlockSpec((tm, kg), lambda i: (i, 0)),
            pl.BlockSpec((tm, km), lambda i: (i, 0)),
            pl.BlockSpec((kg + km, D), lambda i: (0, 0)),
            pl.BlockSpec((tm, D), lambda i: (i, 0)),
            pl.BlockSpec((1, D), lambda i: (0, 0)),
        ],
        out_specs=pl.BlockSpec((tm, D), lambda i: (i, 0)),
        out_shape=jax.ShapeDtypeStruct((T, D), F32),
        compiler_params=pltpu.CompilerParams(
            dimension_semantics=("parallel",),
            vmem_limit_bytes=VMEM_LIMIT_BYTES),
        name="out_proj",
    )(o_gla, o_moba, w, h, norm_w)


def _pack_w_in(w_in):
    lr1 = COL_LR + GLA_GATE_RANK
    left = jnp.pad(w_in[:, :lr1], ((0, 0), (0, IN_PACKED_WIDTH - lr1)))
    right = jnp.pad(w_in[:, lr1:], ((0, 0), (COL_MQ, IN_PACKED_WIDTH - COL_END)))
    return (left + right).astype(BF16)


def kernel(x, ffn1_pre_norm, ffn1_w_gate, ffn1_w_up, ffn1_w_down, ffn1_post_norm, mix_pre_norm, w_in, gla_w_decay_up, gla_b_decay, gla_out_norm, w_out, mix_post_norm, ffn2_pre_norm, ffn2_w_gate, ffn2_w_up, ffn2_w_down, ffn2_post_norm):
    B, S, D = x.shape
    T = B * S
    depth = w_in.shape[0]
    slopes = jnp.exp2(-ALIBI_MAX_BIAS * jnp.arange(1, MOBA_HEADS + 1, dtype=F32) / MOBA_HEADS)
    row = lambda a: a.reshape(1, -1).astype(F32)

    h = x.reshape(T, D)
    for l in range(depth):
        h = _ffn(h, row(ffn1_pre_norm[l]), ffn1_w_gate[l].astype(BF16), ffn1_w_up[l].astype(BF16),
                 ffn1_w_down[l].astype(BF16), row(ffn1_post_norm[l]), tm=FFN_TILE_M, tf=FFN_TILE_F)

        proj = _in_proj(h, row(mix_pre_norm[l]), _pack_w_in(w_in[l]), tm=IN_TILE_M, tn=IN_TILE_N)
        proj = proj.reshape(B, S, IN_PACKED_WIDTH)
        w_up = jnp.concatenate(
            [gla_w_decay_up[l], jnp.zeros((LR_PAD - GLA_GATE_RANK, GLA_QK_WIDTH), F32)], 0).astype(BF16)
        o_gla = _gla(proj, w_up, row(gla_b_decay[l]), row(gla_out_norm[l]))
        o_moba = _moba(proj, slopes)
        h = _out_proj(o_gla.reshape(T, GLA_V_WIDTH), o_moba.reshape(T, MOBA_WIDTH),
                      w_out[l].astype(BF16), h, row(mix_post_norm[l]), tm=OUT_TILE_M)

        h = _ffn(h, row(ffn2_pre_norm[l]), ffn2_w_gate[l].astype(BF16), ffn2_w_up[l].astype(BF16),
                 ffn2_w_down[l].astype(BF16), row(ffn2_post_norm[l]), tm=FFN_TILE_M, tf=FFN_TILE_F)
    return h.reshape(B, S, D)
```

```python
import functools
import math

import jax
import jax.numpy as jnp
import numpy as np
from jax import lax
from jax.experimental import pallas as pl
from jax.experimental.pallas import tpu as pltpu

F32 = jnp.float32
BF16 = jnp.bfloat16

NORM_EPS = 1e-6

GLA_HEADS = 4
GLA_DK = 128
GLA_DV = 256
GLA_CHUNK = 64
GLA_CHUNKS_PER_STEP = 32
FFN_TILE_M = 512
FFN_TILE_F = 512
GLA_GATE_RANK = 16
GLA_GATE_NORMALIZER = 16.0
MOBA_HEADS = 8
MOBA_HEAD_DIM = 128
MOBA_BLOCK = 256
MOBA_TOPK = 3
MOBA_HEADS_PER_STEP = 2
ALIBI_MAX_BIAS = 8.0

GLA_QK_WIDTH = GLA_HEADS * GLA_DK
GLA_V_WIDTH = GLA_HEADS * GLA_DV
MOBA_WIDTH = MOBA_HEADS * MOBA_HEAD_DIM

V7X_LANES = 128
V7X_VMEM_BYTES = 64 * 1024 * 1024
VMEM_LIMIT_BYTES = V7X_VMEM_BYTES - 8 * 1024 * 1024

LR_PAD = V7X_LANES
COL_GQ = 0
COL_GK = COL_GQ + GLA_QK_WIDTH
COL_GV = COL_GK + GLA_QK_WIDTH
COL_GG = COL_GV + GLA_V_WIDTH
COL_LR = COL_GG + GLA_V_WIDTH
COL_MQ = COL_LR + 2 * LR_PAD
COL_MK = COL_MQ + MOBA_WIDTH
COL_MV = COL_MK + MOBA_WIDTH
COL_END = COL_MV + MOBA_WIDTH
IN_TILE_M = 1024
IN_TILE_N = 1280
OUT_TILE_M = 512
IN_PACKED_WIDTH = -(-COL_END // IN_TILE_N) * IN_TILE_N

LOG2E = math.log2(math.e)

_NT = (((1,), (1,)), ((), ()))
_TN = (((0,), (0,)), ((), ()))


def _dot(a, b):
    return jnp.dot(a, b, preferred_element_type=F32)


def _dot_nt(a, b):
    return lax.dot_general(a, b, _NT, preferred_element_type=F32)


def _dot_tn(a, b):
    return lax.dot_general(a, b, _TN, preferred_element_type=F32)


def _rms_norm(x, w):
    ms = jnp.mean(x * x, axis=-1, keepdims=True)
    return x * lax.rsqrt(ms + NORM_EPS) * w


def _silu(x):
    return x / (1.0 + jnp.exp(-x))


def _split3_bf16(x):
    hi = x.astype(BF16)
    r1 = x - hi.astype(F32)
    mid = r1.astype(BF16)
    lo = (r1 - mid.astype(F32)).astype(BF16)
    return hi, mid, lo


def _ffn_kernel(x_ref, pre_ref, wg_ref, wu_ref, wd_ref, post_ref, o_ref, xn_ref):
    j = pl.program_id(1)

    @pl.when(j == 0)
    def _():
        xn_ref[...] = _rms_norm(x_ref[...], pre_ref[...]).astype(BF16)
        o_ref[...] = jnp.zeros_like(o_ref)

    xn = xn_ref[...]
    g = _dot(xn, wg_ref[...])
    u = _dot(xn, wu_ref[...])
    o_ref[...] += _dot((_silu(g) * u).astype(BF16), wd_ref[...])

    @pl.when(j == pl.num_programs(1) - 1)
    def _():
        o_ref[...] = x_ref[...] + 0.5 * _rms_norm(o_ref[...], post_ref[...])


def _ffn(x, pre_w, wg, wu, wd, post_w, *, tm, tf):
    T, D = x.shape
    F = wg.shape[1]
    assert T % tm == 0 and F % tf == 0
    return pl.pallas_call(
        _ffn_kernel,
        grid=(T // tm, F // tf),
        in_specs=[
            pl.BlockSpec((tm, D), lambda i, j: (i, 0)),
            pl.BlockSpec((1, D), lambda i, j: (0, 0)),
            pl.BlockSpec((D, tf), lambda i, j: (0, j)),
            pl.BlockSpec((D, tf), lambda i, j: (0, j)),
            pl.BlockSpec((tf, D), lambda i, j: (j, 0)),
            pl.BlockSpec((1, D), lambda i, j: (0, 0)),
        ],
        out_specs=pl.BlockSpec((tm, D), lambda i, j: (i, 0)),
        out_shape=jax.ShapeDtypeStruct((T, D), F32),
        scratch_shapes=[pltpu.VMEM((tm, D), BF16)],
        compiler_params=pltpu.CompilerParams(
            dimension_semantics=("parallel", "arbitrary"),
            vmem_limit_bytes=VMEM_LIMIT_BYTES),
        name="ffn",
    )(x, pre_w, wg, wu, wd, post_w)


def _in_proj_kernel(x_ref, nw_ref, w_ref, o_ref, xn_ref):
    @pl.when(pl.program_id(1) == 0)
    def _():
        xn_ref[...] = _rms_norm(x_ref[...], nw_ref[...]).astype(BF16)

    o_ref[...] = _dot(xn_ref[...], w_ref[...]).astype(o_ref.dtype)


def _in_proj(x, norm_w, w, *, tm, tn):
    T, D = x.shape
    N = w.shape[1]
    assert T % tm == 0 and N % tn == 0
    return pl.pallas_call(
        _in_proj_kernel,
        grid=(T // tm, N // tn),
        in_specs=[
            pl.BlockSpec((tm, D), lambda i, j: (i, 0)),
            pl.BlockSpec((1, D), lambda i, j: (0, 0)),
            pl.BlockSpec((D, tn), lambda i, j: (0, j)),
        ],
        out_specs=pl.BlockSpec((tm, tn), lambda i, j: (i, j)),
        out_shape=jax.ShapeDtypeStruct((T, N), BF16),
        scratch_shapes=[pltpu.VMEM((tm, D), BF16)],
        compiler_params=pltpu.CompilerParams(
            dimension_semantics=("parallel", "arbitrary"),
            vmem_limit_bytes=VMEM_LIMIT_BYTES),
        name="in_proj",
    )(x, norm_w, w)


def _gla_tables(C):
    levels = int(np.log2(C))
    idx = np.arange(C)
    t = idx[None, :]
    i = idx[:, None]
    sums = [np.tril(np.ones((C, C), np.float32))]
    masks = []
    for l in range(levels):
        b = 1 << l
        r = ((idx // (2 * b)) * (2 * b) + b - 1)[:, None]
        m = np.where((t > r) & (t <= i), 1.0, 0.0) + np.where((t > i) & (t <= r), 1.0, 0.0)
        sums.append(m.astype(np.float32))
        second = (idx // b) % 2 == 1
        same = (i // (2 * b)) == (t // (2 * b))
        masks.append((same & second[:, None] & ~second[None, :]).astype(np.float32))
    masks.append(np.eye(C, dtype=np.float32))
    sums = np.concatenate(sums, 0)
    return np.concatenate([sums, sums, sums], 1), np.stack(masks, 0)


def _gla_kernel(q_ref, k_ref, v_ref, g_ref, lr_ref, wd_ref, bd_ref, nw_ref, sums_ref,
                masks_ref, o_ref, la_ref, state_ref, *, C, U):
    S, dk = q_ref.shape
    levels = masks_ref.shape[0] - 1
    q_scale = dk ** -0.5

    z = _dot(lr_ref[...], wd_ref[...]) + bd_ref[...]
    log_sig = jnp.minimum(z, 0.0) - jnp.log(1.0 + jnp.exp(-jnp.abs(z)))
    la_ref[...] = log_sig * (1.0 / GLA_GATE_NORMALIZER)
    state_ref[...] = jnp.zeros_like(state_ref)

    def step(i, carry):
        base = i * (C * U)
        rows = [pl.multiple_of(base + u * C, C) for u in range(U)]
        qs = [q_ref[pl.ds(r0, C), :].astype(F32) * q_scale for r0 in rows]
        ks = [k_ref[pl.ds(r0, C), :].astype(F32) for r0 in rows]
        vs = [v_ref[pl.ds(r0, C), :] for r0 in rows]
        e_alls = []
        for r0 in rows:
            hi, mid, lo = _split3_bf16(la_ref[pl.ds(r0, C), :])
            e_alls.append(_dot(sums_ref[...], jnp.concatenate([hi, mid, lo], axis=0)))
        scores = [_dot_nt(q.astype(BF16), k.astype(BF16)) * masks_ref[levels]
                  for q, k in zip(qs, ks)]
        for l in range(levels):
            for u in range(U):
                e = jnp.exp(e_alls[u][(l + 1) * C:(l + 2) * C])
                scores[u] += _dot_nt((qs[u] * e).astype(BF16),
                                     (ks[u] * e).astype(BF16)) * masks_ref[l]
        parts = []
        for u in range(U):
            G = e_alls[u][0:C]
            G_last = G[C - 1:C]
            intra = _dot(scores[u].astype(BF16), vs[u])
            q_dec = (qs[u] * jnp.exp(G)).astype(BF16)
            k_dec = (ks[u] * jnp.exp(G_last - G)).astype(BF16)
            kv_t = _dot_tn(vs[u], k_dec)
            parts.append((q_dec, intra, kv_t, jnp.exp(G_last)))

        state_t = state_ref[...]
        for r0, (q_dec, intra, kv_t, decay_last) in zip(rows, parts):
            o = _dot_nt(q_dec, state_t.astype(BF16)) + intra
            gate = g_ref[pl.ds(r0, C), :].astype(F32)
            o_ref[pl.ds(r0, C), :] = (_rms_norm(o, nw_ref[...]) * _silu(gate)).astype(o_ref.dtype)
            state_t = state_t * decay_last + kv_t
        state_ref[...] = state_t
        return carry

    lax.fori_loop(0, S // (C * U), step, 0)


def _gla(proj, w_decay_up, b_decay, norm_w, *, C=GLA_CHUNK, U=GLA_CHUNKS_PER_STEP):
    B, S, _ = proj.shape
    H, dk, dv = GLA_HEADS, GLA_DK, GLA_DV
    assert S % (C * U) == 0
    sums_np, masks_np = _gla_tables(C)
    sums = jnp.asarray(sums_np, BF16)
    masks = jnp.asarray(masks_np, F32)
    kern = functools.partial(_gla_kernel, C=C, U=U)
    return pl.pallas_call(
        kern,
        grid=(B, H),
        in_specs=[
            pl.BlockSpec((None, S, dk), lambda b, h: (b, 0, COL_GQ // dk + h)),
            pl.BlockSpec((None, S, dk), lambda b, h: (b, 0, COL_GK // dk + h)),
            pl.BlockSpec((None, S, dv), lambda b, h: (b, 0, COL_GV // dv + h)),
            pl.BlockSpec((None, S, dv), lambda b, h: (b, 0, COL_GG // dv + h)),
            pl.BlockSpec((None, S, LR_PAD), lambda b, h: (b, 0, COL_LR // LR_PAD)),
            pl.BlockSpec((LR_PAD, dk), lambda b, h: (0, h)),
            pl.BlockSpec((1, dk), lambda b, h: (0, h)),
            pl.BlockSpec((1, dv), lambda b, h: (0, 0)),
            pl.BlockSpec(sums_np.shape, lambda b, h: (0, 0)),
            pl.BlockSpec(masks_np.shape, lambda b, h: (0, 0, 0)),
        ],
        out_specs=pl.BlockSpec((None, S, dv), lambda b, h: (b, 0, h)),
        out_shape=jax.ShapeDtypeStruct((B, S, H * dv), BF16),
        scratch_shapes=[pltpu.VMEM((S, dk), F32), pltpu.VMEM((dv, dk), F32)],
        compiler_params=pltpu.CompilerParams(
            dimension_semantics=("parallel", "parallel"),
            vmem_limit_bytes=VMEM_LIMIT_BYTES),
        name="gla",
    )(proj, proj, proj, proj, proj, w_decay_up, b_decay, norm_w, sums, masks)


def _moba_head(slope, q_ref, k_ref, v_ref, o_ref, vt_ref, ka_ref, qa_ref, t_ref, p_ref,
               *, BS, topk):
    S, hd = q_ref.shape
    NB = S // BS
    scale = hd ** -0.5
    scale2 = scale * LOG2E
    beta = slope * (1.0 / scale)
    neg = -1e30

    vt_ref[...] = v_ref[...].T

    lane = lax.broadcasted_iota(jnp.int32, (S, hd), 1)
    k_off = (lax.broadcasted_iota(jnp.int32, (S, hd), 0) & (BS - 1)).astype(F32)
    ka_ref[:, 0:hd] = k_ref[...]
    ka_ref[:, hd:] = jnp.where(lane < 3, k_off, 0.0).astype(BF16)
    b_hi, b_mid, b_lo = _split3_bf16(jnp.full((hd, S), beta, F32))
    row = lax.broadcasted_iota(jnp.int32, (hd, S), 0)
    qa_ref[0:hd, :] = q_ref[...].T
    qa_ref[hd:, :] = jnp.where(row == 0, b_hi.astype(F32), jnp.where(
        row == 1, b_mid.astype(F32), jnp.where(row == 2, b_lo.astype(F32), 0.0))).astype(BF16)

    kmean = jnp.sum(k_ref[...].astype(F32).reshape(NB, BS, hd), axis=1) * (1.0 / BS)
    hi, mid, lo = _split3_bf16(kmean)
    kparts = jnp.concatenate([hi.astype(F32), mid.astype(F32), lo.astype(F32)], 0).astype(BF16)

    causal = (lax.broadcasted_iota(jnp.int32, (BS, BS), 0)
              <= lax.broadcasted_iota(jnp.int32, (BS, BS), 1))

    def base(n):
        return BS * (n * (n + 1) // 2)

    def pass1(n):
        cols = slice(n * BS, (n + 1) * BS)
        sel = [None] * (n + 1)
        if n > topk:
            gp = _dot(kparts, qa_ref[0:hd, cols])
            gate = [gp[j:j + 1] + gp[NB + j:NB + j + 1] + gp[2 * NB + j:2 * NB + j + 1]
                    for j in range(n)]
            for j in range(n):
                rank = jnp.zeros((1, BS), F32)
                for jj in range(n):
                    if jj == j:
                        continue
                    ahead = (gate[jj] >= gate[j]) if jj < j else (gate[jj] > gate[j])
                    rank += ahead.astype(F32)
                sel[j] = rank < float(topk)

        L = (n + 1) * BS
        t_ref[base(n):base(n) + L, :] = _dot(ka_ref[0:L, :], qa_ref[:, cols])
        own = slice(base(n) + n * BS, base(n) + L)
        t_ref[own, :] = jnp.where(causal, t_ref[own, :], neg)
        offsets = [beta * float((n - j) * BS) for j in range(n + 1)]
        m = None
        for j in range(n + 1):
            t = t_ref[base(n) + j * BS:base(n) + (j + 1) * BS, :]
            mj = jnp.max(t, axis=0, keepdims=True) - offsets[j]
            if sel[j] is not None:
                mj = jnp.where(sel[j], mj, neg)
            m = mj if m is None else jnp.maximum(m, mj)
        return sel, offsets, m

    def pass2(n, sel, offsets, m):
        L = (n + 1) * BS
        l = jnp.zeros((1, BS), F32)
        for j in range(n + 1):
            rows = slice(base(n) + j * BS, base(n) + (j + 1) * BS)
            c = (m + offsets[j]) * scale2
            if sel[j] is not None:
                c = jnp.where(sel[j], c, -neg)
            p = jnp.exp2(t_ref[rows, :] * scale2 - c)
            l += jnp.sum(p, axis=0, keepdims=True)
            p_ref[rows, :] = p.astype(BF16)
        acc = _dot(vt_ref[:, 0:L], p_ref[base(n):base(n) + L, :])
        o_ref[n * BS:(n + 1) * BS, :] = (acc / l).T.astype(o_ref.dtype)

    return pass1, pass2


def _moba_kernel(slopes_ref, q_ref, k_ref, v_ref, o_ref, vt_ref, ka_ref, qa_ref, t_ref, p_ref,
                 *, BS, topk, heads):
    hd = q_ref.shape[1] // heads
    NB = q_ref.shape[0] // BS
    passes = []
    for hh in range(heads):
        cols = slice(hh * hd, (hh + 1) * hd)
        passes.append(_moba_head(
            slopes_ref[pl.program_id(1) * heads + hh], q_ref.at[:, cols], k_ref.at[:, cols],
            v_ref.at[:, cols], o_ref.at[:, cols], vt_ref.at[hh], ka_ref.at[hh], qa_ref.at[hh],
            t_ref.at[hh], p_ref.at[hh], BS=BS, topk=topk))
    stats = [pass1(0) for pass1, _ in passes]
    for n in range(NB):
        nxt = [pass1(n + 1) if n + 1 < NB else None for pass1, _ in passes]
        for (_, pass2), st in zip(passes, stats):
            pass2(n, *st)
        stats = nxt


def _moba(proj, slopes, *, heads=MOBA_HEADS_PER_STEP):
    B, S, _ = proj.shape
    H, hd, BS = MOBA_HEADS, MOBA_HEAD_DIM, MOBA_BLOCK
    assert S % BS == 0
    NB = S // BS
    tile_rows = BS * (NB * (NB + 1) // 2)
    kern = functools.partial(_moba_kernel, BS=BS, topk=MOBA_TOPK, heads=heads)
    w = heads * hd
    assert H % heads == 0 and COL_MQ % w == 0 and COL_MK % w == 0 and COL_MV % w == 0
    grid_spec = pltpu.PrefetchScalarGridSpec(
        num_scalar_prefetch=1,
        grid=(B, H // heads),
        in_specs=[
            pl.BlockSpec((None, S, w), lambda b, h, s: (b, 0, COL_MQ // w + h)),
            pl.BlockSpec((None, S, w), lambda b, h, s: (b, 0, COL_MK // w + h)),
            pl.BlockSpec((None, S, w), lambda b, h, s: (b, 0, COL_MV // w + h)),
        ],
        out_specs=pl.BlockSpec((None, S, w), lambda b, h, s: (b, 0, h)),
        scratch_shapes=[pltpu.VMEM((heads, hd, S), BF16), pltpu.VMEM((heads, S, 2 * hd), BF16),
                        pltpu.VMEM((heads, 2 * hd, S), BF16),
                        pltpu.VMEM((heads, tile_rows, BS), F32),
                        pltpu.VMEM((heads, tile_rows, BS), BF16)],
    )
    return pl.pallas_call(
        kern,
        grid_spec=grid_spec,
        out_shape=jax.ShapeDtypeStruct((B, S, H * hd), BF16),
        compiler_params=pltpu.CompilerParams(
            dimension_semantics=("parallel", "parallel"),
            vmem_limit_bytes=VMEM_LIMIT_BYTES),
        name="moba",
    )(slopes, proj, proj, proj)


def _mixers_kernel(slopes_ref, gq_ref, gk_ref, gv_ref, gg_ref, lr_ref, wd_ref, bd_ref, nw_ref,
                   sums_ref, masks_ref, mq_ref, mk_ref, mv_ref, og_ref, om_ref,
                   la_ref, state_ref, vt_ref, ka_ref, qa_ref, t_ref, p_ref,
                   *, C, U, BS, topk, heads):
    _gla_kernel(gq_ref, gk_ref, gv_ref, gg_ref, lr_ref, wd_ref, bd_ref, nw_ref, sums_ref,
                masks_ref, og_ref, la_ref, state_ref, C=C, U=U)
    _moba_kernel(slopes_ref, mq_ref, mk_ref, mv_ref, om_ref, vt_ref, ka_ref, qa_ref, t_ref,
                 p_ref, BS=BS, topk=topk, heads=heads)


def _mixers(proj, w_decay_up, b_decay, norm_w, slopes, *, C=GLA_CHUNK, U=GLA_CHUNKS_PER_STEP,
            heads=MOBA_HEADS_PER_STEP):
    B, S, _ = proj.shape
    H, dk, dv = GLA_HEADS, GLA_DK, GLA_DV
    hd, BS = MOBA_HEAD_DIM, MOBA_BLOCK
    assert MOBA_HEADS // heads == H and S % (C * U) == 0 and S % BS == 0
    NB = S // BS
    tile_rows = BS * (NB * (NB + 1) // 2)
    w = heads * hd
    sums_np, masks_np = _gla_tables(C)
    kern = functools.partial(_mixers_kernel, C=C, U=U, BS=BS, topk=MOBA_TOPK, heads=heads)
    grid_spec = pltpu.PrefetchScalarGridSpec(
        num_scalar_prefetch=1,
        grid=(B, H),
        in_specs=[
            pl.BlockSpec((None, S, dk), lambda b, h, s: (b, 0, COL_GQ // dk + h)),
            pl.BlockSpec((None, S, dk), lambda b, h, s: (b, 0, COL_GK // dk + h)),
            pl.BlockSpec((None, S, dv), lambda b, h, s: (b, 0, COL_GV // dv + h)),
            pl.BlockSpec((None, S, dv), lambda b, h, s: (b, 0, COL_GG // dv + h)),
            pl.BlockSpec((None, S, LR_PAD), lambda b, h, s: (b, 0, COL_LR // LR_PAD)),
            pl.BlockSpec((LR_PAD, dk), lambda b, h, s: (0, h)),
            pl.BlockSpec((1, dk), lambda b, h, s: (0, h)),
            pl.BlockSpec((1, dv), lambda b, h, s: (0, 0)),
            pl.BlockSpec(sums_np.shape, lambda b, h, s: (0, 0)),
            pl.BlockSpec(masks_np.shape, lambda b, h, s: (0, 0, 0)),
            pl.BlockSpec((None, S, w), lambda b, h, s: (b, 0, COL_MQ // w + h)),
            pl.BlockSpec((None, S, w), lambda b, h, s: (b, 0, COL_MK // w + h)),
            pl.BlockSpec((None, S, w), lambda b, h, s: (b, 0, COL_MV // w + h)),
        ],
        out_specs=[pl.BlockSpec((None, S, dv), lambda b, h, s: (b, 0, h)),
                   pl.BlockSpec((None, S, w), lambda b, h, s: (b, 0, h))],
        scratch_shapes=[pltpu.VMEM((S, dk), F32), pltpu.VMEM((dv, dk), F32),
                        pltpu.VMEM((heads, hd, S), BF16), pltpu.VMEM((heads, S, 2 * hd), BF16),
                        pltpu.VMEM((heads, 2 * hd, S), BF16),
                        pltpu.VMEM((heads, tile_rows, BS), F32),
                        pltpu.VMEM((heads, tile_rows, BS), BF16)],
    )
    return pl.pallas_call(
        kern,
        grid_spec=grid_spec,
        out_shape=[jax.ShapeDtypeStruct((B, S, H * dv), BF16),
                   jax.ShapeDtypeStruct((B, S, MOBA_HEADS * hd), BF16)],
        compiler_params=pltpu.CompilerParams(
            dimension_semantics=("parallel", "parallel"),
            vmem_limit_bytes=VMEM_LIMIT_BYTES),
        name="mixers",
    )(slopes, proj, proj, proj, proj, proj, w_decay_up, b_decay, norm_w,
      jnp.asarray(sums_np, BF16), jnp.asarray(masks_np, F32), proj, proj, proj)


def _out_proj_kernel(og_ref, om_ref, w_ref, h_ref, nw_ref, o_ref):
    kg = og_ref.shape[1]
    m = _dot(og_ref[...], w_ref[0:kg, :]) + _dot(om_ref[...], w_ref[kg:, :])
    o_ref[...] = h_ref[...] + _rms_norm(m, nw_ref[...])


def _out_proj(o_gla, o_moba, w, h, norm_w, *, tm):
    T, D = h.shape
    kg, km = o_gla.shape[1], o_moba.shape[1]
    assert T % tm == 0
    return pl.pallas_call(
        _out_proj_kernel,
        grid=(T // tm,),
        in_specs=[
            pl.BlockSpec((tm, kg), lambda i: (i, 0)),
            pl.BlockSpec((tm, km), lambda i: (i, 0)),
            pl.BlockSpec((kg + km, D), lambda i: (0, 0)),
            pl.BlockSpec((tm, D), lambda i: (i, 0)),
            pl.BlockSpec((1, D), lambda i: (0, 0)),
        ],
        out_specs=pl.BlockSpec((tm, D), lambda i: (i, 0)),
        out_shape=jax.ShapeDtypeStruct((T, D), F32),
        compiler_params=pltpu.CompilerParams(
            dimension_semantics=("parallel",),
            vmem_limit_bytes=VMEM_LIMIT_BYTES),
        name="out_proj",
    )(o_gla, o_moba, w, h, norm_w)


def _pack_w_in(w_in):
    lr1 = COL_LR + GLA_GATE_RANK
    left = jnp.pad(w_in[:, :lr1], ((0, 0), (0, IN_PACKED_WIDTH - lr1)))
    right = jnp.pad(w_in[:, lr1:], ((0, 0), (COL_MQ, IN_PACKED_WIDTH - COL_END)))
    return (left + right).astype(BF16)


def kernel(x, ffn1_pre_norm, ffn1_w_gate, ffn1_w_up, ffn1_w_down, ffn1_post_norm, mix_pre_norm, w_in, gla_w_decay_up, gla_b_decay, gla_out_norm, w_out, mix_post_norm, ffn2_pre_norm, ffn2_w_gate, ffn2_w_up, ffn2_w_down, ffn2_post_norm):
    B, S, D = x.shape
    T = B * S
    depth = w_in.shape[0]
    slopes = jnp.exp2(-ALIBI_MAX_BIAS * jnp.arange(1, MOBA_HEADS + 1, dtype=F32) / MOBA_HEADS)
    row = lambda a: a.reshape(1, -1).astype(F32)

    h = x.reshape(T, D)
    for l in range(depth):
        h = _ffn(h, row(ffn1_pre_norm[l]), ffn1_w_gate[l].astype(BF16), ffn1_w_up[l].astype(BF16),
                 ffn1_w_down[l].astype(BF16), row(ffn1_post_norm[l]), tm=FFN_TILE_M, tf=FFN_TILE_F)

        proj = _in_proj(h, row(mix_pre_norm[l]), _pack_w_in(w_in[l]), tm=IN_TILE_M, tn=IN_TILE_N)
        proj = proj.reshape(B, S, IN_PACKED_WIDTH)
        w_up = jnp.concatenate(
            [gla_w_decay_up[l], jnp.zeros((LR_PAD - GLA_GATE_RANK, GLA_QK_WIDTH), F32)], 0).astype(BF16)
        o_gla, o_moba = _mixers(proj, w_up, row(gla_b_decay[l]), row(gla_out_norm[l]), slopes)
        h = _out_proj(o_gla.reshape(T, GLA_V_WIDTH), o_moba.reshape(T, MOBA_WIDTH),
                      w_out[l].astype(BF16), h, row(mix_post_norm[l]), tm=OUT_TILE_M)

        h = _ffn(h, row(ffn2_pre_norm[l]), ffn2_w_gate[l].astype(BF16), ffn2_w_up[l].astype(BF16),
                 ffn2_w_down[l].astype(BF16), row(ffn2_post_norm[l]), tm=FFN_TILE_M, tf=FFN_TILE_F)
    return h.reshape(B, S, D)
```
